```python
import math
import jax, jax.numpy as jnp
from jax import lax
import numpy as np

D_MODEL = 2048
BATCH = 16
SEQ = 256
DEPTH = 4
DEC_BATCH = 8
DEC_SEQ = 4096
PAST_LEN = 512

GRID_W = 64
N_MIXERS = 3
N_ATTN = (DEPTH + 2) // 3
N_SSD = (DEPTH + 1) // 3
N_LRU = DEPTH // 3
HEAD_DIM = 128
N_HEADS = D_MODEL // HEAD_DIM
N_KV_HEADS = 4
Q_PER_KV = N_HEADS // N_KV_HEADS
QKV_DIM = (N_HEADS + 2 * N_KV_HEADS) * HEAD_DIM
ROPE_THETA = 10000.0
Q_BLOCK = 128
SSD_D_INNER = 2 * D_MODEL
SSD_HEAD_DIM = 64
SSD_HEADS = SSD_D_INNER // SSD_HEAD_DIM
SSD_GROUPS = 8
SSD_HEADS_PER_GROUP = SSD_HEADS // SSD_GROUPS
SSD_STATE = 128
SSD_CHUNK = 128
SSD_CONV_DIM = SSD_D_INNER + 2 * SSD_GROUPS * SSD_STATE
SSD_IN_DIM = SSD_D_INNER + SSD_CONV_DIM + 2 * SSD_HEADS
CONV_WIDTH = 4
LRU_WIDTH = D_MODEL
LRU_BLOCKS = 16
LRU_BLOCK_W = LRU_WIDTH // LRU_BLOCKS
LRU_C = 8.0
D_FF = -(-8 * D_MODEL // (3 * 256)) * 256
EPS = 1e-6

kernel_name = 'hybrid_prefix_diffusion_step'

F32 = jnp.float32


def rms_norm(x, g):
    xf = x.astype(F32)
    y = xf * lax.rsqrt(jnp.mean(xf * xf, axis=-1, keepdims=True) + EPS)
    return (y * g.astype(F32)).astype(x.dtype)


def adaln(cond, w, b):
    m = (jax.nn.silu(cond) @ w + b)[:, None, :]
    return jnp.split(m, 6, axis=-1)


def modulate(h, shift, scale):
    return h * (1 + scale) + shift


def axial_rope(rows):
    row = jnp.repeat(jnp.arange(rows, dtype=F32), GRID_W)
    col = jnp.tile(jnp.arange(GRID_W, dtype=F32), rows)
    n_freq = HEAD_DIM // 4
    inv = ROPE_THETA ** (-jnp.arange(n_freq, dtype=F32) / n_freq)
    ang = jnp.concatenate([row[:, None] * inv, col[:, None] * inv], axis=-1)
    return jnp.cos(ang), jnp.sin(ang)


def apply_rope(x, cos, sin):
    xf = x.astype(F32)
    x1, x2 = jnp.split(xf, 2, axis=-1)
    c, s = cos[None, :, None, :], sin[None, :, None, :]
    return jnp.concatenate([x1 * c - x2 * s, x2 * c + x1 * s], axis=-1).astype(x.dtype)


def blocked_attention(q, k, v):
    b, lq = q.shape[:2]
    nb = lq // Q_BLOCK
    qb = jnp.moveaxis(q.reshape((b, nb, Q_BLOCK) + q.shape[2:]), 1, 0)
    scale = HEAD_DIM ** -0.5

    def one_block(q_blk):
        s = jnp.einsum('bqkgd,bskd->bkgqs', q_blk, k).astype(F32) * scale
        p = jax.nn.softmax(s, axis=-1).astype(v.dtype)
        return jnp.einsum('bkgqs,bskd->bqkgd', p, v)

    o = lax.map(one_block, qb)
    return jnp.moveaxis(o, 0, 1).reshape(b, lq, N_HEADS * HEAD_DIM)


def attention_mixer(h, w_qkv, q_norm, k_norm, w_o, rope=None, k_ctx=None, v_ctx=None):
    b, l, _ = h.shape
    q, k, v = jnp.split(h @ w_qkv, [N_HEADS * HEAD_DIM, (N_HEADS + N_KV_HEADS) * HEAD_DIM], axis=-1)
    q = rms_norm(q.reshape(b, l, N_HEADS, HEAD_DIM), q_norm)
    k = rms_norm(k.reshape(b, l, N_KV_HEADS, HEAD_DIM), k_norm)
    v = v.reshape(b, l, N_KV_HEADS, HEAD_DIM)
    if rope is None:
        k_all, v_all = k, v
    else:
        q = apply_rope(q, *rope)
        k = apply_rope(k, *rope)
        k_all = jnp.concatenate([k, k_ctx], axis=1)
        v_all = jnp.concatenate([v, v_ctx], axis=1)
    o = blocked_attention(q.reshape(b, l, N_KV_HEADS, Q_PER_KV, HEAD_DIM), k_all, v_all)
    return o @ w_o, k, v


def centred_dwconv(x, w, bias):
    l = x.shape[1]
    left = (CONV_WIDTH - 1) // 2
    xp = jnp.pad(x, ((0, 0), (left, CONV_WIDTH - 1 - left), (0, 0)))
    return sum(xp[:, t:t + l] * w[t] for t in range(CONV_WIDTH)) + bias


def ssd_chunk_scan(x, dt, a, bm, cm, h0):
    b, l = x.shape[:2]
    nc = l // SSD_CHUNK
    G, HG = SSD_GROUPS, SSD_HEADS_PER_GROUP

    def to_chunks(t):
        return jnp.swapaxes(t.reshape((b, nc, SSD_CHUNK) + t.shape[2:]), 0, 1)

    xc = to_chunks(x.reshape(b, l, G, HG, SSD_HEAD_DIM))
    dtc = to_chunks(dt.reshape(b, l, G, HG))
    bc, cc = to_chunks(bm), to_chunks(cm)
    ag = a.reshape(G, HG)
    causal = jnp.tril(jnp.ones((SSD_CHUNK, SSD_CHUNK), dtype=bool))[None, :, :, None, None]

    def step(state, inp):
        x_k, dt_k, b_k, c_k = inp
        cum = jnp.cumsum(dt_k * ag, axis=1)
        seg = cum[:, :, None] - cum[:, None, :]
        decay = jnp.exp(jnp.where(causal, seg, -jnp.inf))
        scores = jnp.einsum('bign,bjgn->bijg', c_k, b_k)
        xdt = x_k * dt_k[..., None]
        y_intra = jnp.einsum('bijg,bijgh,bjghp->bighp', scores, decay, xdt)
        y_inter = jnp.einsum('bign,bghpn->bighp', c_k, state) * jnp.exp(cum)[..., None]
        to_end = jnp.exp(cum[:, -1:] - cum)
        state = state * jnp.exp(cum[:, -1])[..., None, None] + jnp.einsum('bjgh,bjghp,bjgn->bghpn', to_end, xdt, b_k)
        return state, y_intra + y_inter

    state0 = h0.astype(F32).reshape(b, G, HG, SSD_HEAD_DIM, SSD_STATE)
    h_T, ys = lax.scan(step, state0, (xc, dtc, bc, cc))
    y = jnp.swapaxes(ys, 0, 1).reshape(b, l, SSD_HEADS, SSD_HEAD_DIM)
    return y, h_T.reshape(b, SSD_HEADS, SSD_HEAD_DIM, SSD_STATE)


def ssd_mixer(h, h0_f, h0_b, w_in, conv_w, conv_b, dt_bias, a_log, d_skip, norm_w, w_out):
    b, l, _ = h.shape
    z, xbc, dt_raw = jnp.split(h @ w_in, [SSD_D_INNER, SSD_D_INNER + SSD_CONV_DIM], axis=-1)
    xbc = jax.nn.silu(centred_dwconv(xbc, conv_w, conv_b))
    xv, bm, cm = jnp.split(xbc, [SSD_D_INNER, SSD_D_INNER + SSD_GROUPS * SSD_STATE], axis=-1)
    xv = xv.reshape(b, l, SSD_HEADS, SSD_HEAD_DIM).astype(F32)
    bm = bm.reshape(b, l, SSD_GROUPS, SSD_STATE).astype(F32)
    cm = cm.reshape(b, l, SSD_GROUPS, SSD_STATE).astype(F32)
    dt = jax.nn.softplus(dt_raw.reshape(b, l, 2, SSD_HEADS).astype(F32) + dt_bias.astype(F32))
    a = -jnp.exp(a_log.astype(F32))
    y_f, h_f = ssd_chunk_scan(xv, dt[:, :, 0], a[0], bm, cm, h0_f)
    flip = lambda t: jnp.flip(t, axis=1)
    y_b, h_b = ssd_chunk_scan(flip(xv), flip(dt[:, :, 1]), a[1], flip(bm), flip(cm), h0_b)
    y = y_f + flip(y_b) + xv * d_skip.astype(F32)[:, None]
    y = y.reshape(b, l, SSD_D_INNER).astype(h.dtype)
    y = rms_norm(y * jax.nn.silu(z), norm_w)
    return y @ w_out, h_f.astype(h.dtype), h_b.astype(h.dtype)


def linear_recurrence(a, u, h0, reverse):
    first = -1 if reverse else 0
    u = u.at[:, first].add(a[:, first] * h0)

    def combine(e1, e2):
        a1, u1 = e1
        a2, u2 = e2
        return a1 * a2, a2 * u1 + u2

    _, hs = lax.associative_scan(combine, (a, u), reverse=reverse, axis=1)
    return hs, hs[:, 0] if reverse else hs[:, -1]


def rglru_mixer(h, h0_f, h0_b, w_in, conv_w, conv_b, w_a, b_a, w_i, b_i, a_param, w_out):
    b, l, _ = h.shape
    gate_branch, rec = jnp.split(h @ w_in, 2, axis=-1)
    rec = centred_dwconv(rec, conv_w, conv_b)
    rec_blk = rec.reshape(b, l, LRU_BLOCKS, LRU_BLOCK_W)
    rec32 = rec.astype(F32)
    outs, finals = [], []
    for d, h0 in enumerate((h0_f, h0_b)):
        r = jax.nn.sigmoid((jnp.einsum('blnw,nwv->blnv', rec_blk, w_a[d]).reshape(b, l, LRU_WIDTH) + b_a[d]).astype(F32))
        ig = jax.nn.sigmoid((jnp.einsum('blnw,nwv->blnv', rec_blk, w_i[d]).reshape(b, l, LRU_WIDTH) + b_i[d]).astype(F32))
        log_a = -LRU_C * r * jax.nn.softplus(-a_param[d].astype(F32))
        a = jnp.exp(log_a)
        u = jnp.sqrt(-jnp.expm1(2.0 * log_a)) * (ig * rec32)
        hs, h_T = linear_recurrence(a, u, h0.astype(F32), reverse=(d == 1))
        outs.append(hs)
        finals.append(h_T.astype(h.dtype))
    y = (outs[0] + outs[1]).astype(h.dtype) * jax.nn.gelu(gate_branch)
    return y @ w_out, finals[0], finals[1]


def swiglu_ffn(h, w_in, w_out):
    g, u = jnp.split(h @ w_in, 2, axis=-1)
    return (jax.nn.silu(g) * u) @ w_out


def setup_inputs(seed: int = 0) -> dict:
    key = jax.random.key(seed)
    keys = iter(jax.random.split(key, 48))
    D = D_MODEL

    def normal(shape, scale):
        return jax.random.normal(next(keys), shape, F32) * scale

    def uniform(shape, lo, hi):
        return jax.random.uniform(next(keys), shape, F32, lo, hi)

    x_prompt = normal((BATCH, SEQ, D), 1.0)
    x_sample = normal((DEC_BATCH, DEC_SEQ, D), 1.0)
    cache_k = normal((DEC_BATCH, N_ATTN, PAST_LEN, N_KV_HEADS, HEAD_DIM), 1.0)
    cache_v = normal((DEC_BATCH, N_ATTN, PAST_LEN, N_KV_HEADS, HEAD_DIM), 1.0)
    state_ssm = normal((DEC_BATCH, N_SSD, 2, SSD_HEADS, SSD_HEAD_DIM, SSD_STATE), 0.1)
    state_lru = normal((DEC_BATCH, N_LRU, 2, LRU_WIDTH), 0.5)
    c = normal((DEC_BATCH, D), 1.0)
    c_ctx = normal((D,), 1.0)
    ada_w = normal((DEPTH, D, 6 * D), 0.5 * D ** -0.5)
    ada_b = normal((DEPTH, 6 * D), 0.02)
    norm_mix = 1.0 + normal((DEPTH, D), 0.02)
    norm_ffn = 1.0 + normal((DEPTH, D), 0.02)
    attn_w_qkv = normal((N_ATTN, D, QKV_DIM), D ** -0.5)
    attn_q_norm = 1.0 + normal((N_ATTN, HEAD_DIM), 0.02)
    attn_k_norm = 1.0 + normal((N_ATTN, HEAD_DIM), 0.02)
    attn_w_o = normal((N_ATTN, N_HEADS * HEAD_DIM, D), (N_HEADS * HEAD_DIM) ** -0.5)
    ssd_w_in = normal((N_SSD, D, SSD_IN_DIM), D ** -0.5)
    ssd_conv_w = normal((N_SSD, CONV_WIDTH, SSD_CONV_DIM), CONV_WIDTH ** -0.5)
    ssd_conv_b = normal((N_SSD, SSD_CONV_DIM), 0.02)
    dt0 = jnp.exp(uniform((N_SSD, 2, SSD_HEADS), math.log(1e-3), math.log(1e-1)))
    ssd_dt_bias = dt0 + jnp.log(-jnp.expm1(-dt0))
    ssd_a_log = jnp.log(uniform((N_SSD, 2, SSD_HEADS), 1.0, 16.0))
    ssd_d = 1.0 + normal((N_SSD, SSD_HEADS), 0.1)
    ssd_norm = 1.0 + normal((N_SSD, SSD_D_INNER), 0.02)
    ssd_w_out = normal((N_SSD, SSD_D_INNER, D), SSD_D_INNER ** -0.5)
    lru_w_in = normal((N_LRU, D, 2 * LRU_WIDTH), D ** -0.5)
    lru_conv_w = normal((N_LRU, CONV_WIDTH, LRU_WIDTH), CONV_WIDTH ** -0.5)
    lru_conv_b = normal((N_LRU, LRU_WIDTH), 0.02)
    lru_w_a = normal((N_LRU, 2, LRU_BLOCKS, LRU_BLOCK_W, LRU_BLOCK_W), LRU_BLOCK_W ** -0.5)
    lru_b_a = normal((N_LRU, 2, LRU_WIDTH), 0.02)
    lru_w_i = normal((N_LRU, 2, LRU_BLOCKS, LRU_BLOCK_W, LRU_BLOCK_W), LRU_BLOCK_W ** -0.5)
    lru_b_i = normal((N_LRU, 2, LRU_WIDTH), 0.02)
    a0 = uniform((N_LRU, 2, LRU_WIDTH), 0.9, 0.999)
    s0 = a0 ** (1.0 / LRU_C)
    lru_a_param = jnp.log(s0) - jnp.log1p(-s0)
    lru_w_out = normal((N_LRU, LRU_WIDTH, D), LRU_WIDTH ** -0.5)
    ffn_w_in = normal((DEPTH, D, 2 * D_FF), D ** -0.5)
    ffn_w_out = normal((DEPTH, D_FF, D), D_FF ** -0.5)
    final_norm = 1.0 + normal((D,), 0.02)
    return {'x_prompt': x_prompt, 'x_sample': x_sample, 'cache_k': cache_k, 'cache_v': cache_v,
            'state_ssm': state_ssm, 'state_lru': state_lru, 'c': c, 'c_ctx': c_ctx,
            'ada_w': ada_w, 'ada_b': ada_b, 'norm_mix': norm_mix, 'norm_ffn': norm_ffn,
            'attn_w_qkv': attn_w_qkv, 'attn_q_norm': attn_q_norm, 'attn_k_norm': attn_k_norm, 'attn_w_o': attn_w_o,
            'ssd_w_in': ssd_w_in, 'ssd_conv_w': ssd_conv_w, 'ssd_conv_b': ssd_conv_b, 'ssd_dt_bias': ssd_dt_bias,
            'ssd_a_log': ssd_a_log, 'ssd_d': ssd_d, 'ssd_norm': ssd_norm, 'ssd_w_out': ssd_w_out,
            'lru_w_in': lru_w_in, 'lru_conv_w': lru_conv_w, 'lru_conv_b': lru_conv_b, 'lru_w_a': lru_w_a,
            'lru_b_a': lru_b_a, 'lru_w_i': lru_w_i, 'lru_b_i': lru_b_i, 'lru_a_param': lru_a_param,
            'lru_w_out': lru_w_out, 'ffn_w_in': ffn_w_in, 'ffn_w_out': ffn_w_out, 'final_norm': final_norm}


def reference(x_prompt, x_sample, cache_k, cache_v, state_ssm, state_lru, c, c_ctx,
              ada_w, ada_b, norm_mix, norm_ffn,
              attn_w_qkv, attn_q_norm, attn_k_norm, attn_w_o,
              ssd_w_in, ssd_conv_w, ssd_conv_b, ssd_dt_bias, ssd_a_log, ssd_d, ssd_norm, ssd_w_out,
              lru_w_in, lru_conv_w, lru_conv_b, lru_w_a, lru_b_a, lru_w_i, lru_b_i, lru_a_param, lru_w_out,
              ffn_w_in, ffn_w_out, final_norm):
    xp, xs = x_prompt, x_sample
    bp = xp.shape[0]
    rows = xs.shape[1] // GRID_W
    rope = axial_rope(rows)
    new_k, new_v, new_ssm, new_lru = [], [], [], []
    for i in range(DEPTH):
        kind, j = i % N_MIXERS, i // N_MIXERS
        p_sh1, p_sc1, p_g1, p_sh2, p_sc2, p_g2 = adaln(c_ctx[None, :], ada_w[i], ada_b[i])
        s_sh1, s_sc1, s_g1, s_sh2, s_sc2, s_g2 = adaln(c, ada_w[i], ada_b[i])
        hp = modulate(rms_norm(xp, norm_mix[i]), p_sh1, p_sc1)
        hs = modulate(rms_norm(xs, norm_mix[i]), s_sh1, s_sc1)
        if kind == 0:
            op, k_ctx, v_ctx = attention_mixer(hp, attn_w_qkv[j], attn_q_norm[j], attn_k_norm[j], attn_w_o[j])
            os_, _, _ = attention_mixer(hs, attn_w_qkv[j], attn_q_norm[j], attn_k_norm[j], attn_w_o[j],
                                        rope=rope, k_ctx=cache_k[:, j], v_ctx=cache_v[:, j])
            new_k.append(k_ctx)
            new_v.append(v_ctx)
        elif kind == 1:
            zero = jnp.zeros((bp, SSD_HEADS, SSD_HEAD_DIM, SSD_STATE), F32)
            w = (ssd_w_in[j], ssd_conv_w[j], ssd_conv_b[j], ssd_dt_bias[j], ssd_a_log[j], ssd_d[j], ssd_norm[j], ssd_w_out[j])
            op, h_f, h_b = ssd_mixer(hp, zero, zero, *w)
            os_, _, _ = ssd_mixer(hs, state_ssm[:, j, 0], state_ssm[:, j, 1], *w)
            new_ssm.append(jnp.stack([h_f, h_b], axis=1))
        else:
            zero = jnp.zeros((bp, LRU_WIDTH), F32)
            w = (lru_w_in[j], lru_conv_w[j], lru_conv_b[j], lru_w_a[j], lru_b_a[j], lru_w_i[j], lru_b_i[j], lru_a_param[j], lru_w_out[j])
            op, h_f, h_b = rglru_mixer(hp, zero, zero, *w)
            os_, _, _ = rglru_mixer(hs, state_lru[:, j, 0], state_lru[:, j, 1], *w)
            new_lru.append(jnp.stack([h_f, h_b], axis=1))
        xp = xp + p_g1 * op
        xs = xs + s_g1 * os_
        xp = xp + p_g2 * swiglu_ffn(modulate(rms_norm(xp, norm_ffn[i]), p_sh2, p_sc2), ffn_w_in[i], ffn_w_out[i])
        xs = xs + s_g2 * swiglu_ffn(modulate(rms_norm(xs, norm_ffn[i]), s_sh2, s_sc2), ffn_w_in[i], ffn_w_out[i])
    y_prompt = rms_norm(xp, final_norm)
    y_sample = rms_norm(xs, final_norm)
    new_cache_k = jnp.stack(new_k, axis=1)
    new_cache_v = jnp.stack(new_v, axis=1)
    new_state_ssm = jnp.stack(new_ssm, axis=1)
    new_state_lru = jnp.stack(new_lru, axis=1)
    return (y_prompt, y_sample, new_cache_k, new_cache_v, new_state_ssm, new_state_lru)
```

```python
import functools
import math

import jax
import jax.numpy as jnp
from jax import lax
from jax.experimental import pallas as pl
from jax.experimental.pallas import tpu as pltpu

F32 = jnp.float32
BF16 = jnp.bfloat16
EPS = 1e-6

GRID_W = 64
HEAD_DIM = 128
N_KV_HEADS = 4
Q_PER_KV = 4
ROPE_THETA = 10000.0
SSD_HEAD_DIM = 64
SSD_GROUPS = 8
SSD_STATE = 128
SSD_CHUNK = 128
LRU_BLOCK_W = 128
LRU_C = 8.0

LANES = 128
SUBLANES = 8
VMEM_LIMIT_BYTES = 56 * 1024 * 1024


def _params(*sem):
    return pltpu.CompilerParams(dimension_semantics=sem, vmem_limit_bytes=VMEM_LIMIT_BYTES)


def _sigmoid(x):
    return 1.0 / (1.0 + jnp.exp(-x))


def _silu(x):
    return x * _sigmoid(x)


def _softplus(x):
    return jnp.maximum(x, 0.0) + jnp.log(1.0 + jnp.exp(-jnp.abs(x)))


def _dot(a, b):
    return jnp.dot(a, b, preferred_element_type=F32)


def _split3(v):
    v1 = v.astype(BF16)
    r1 = v - v1.astype(F32)
    v2 = r1.astype(BF16)
    v3 = (r1 - v2.astype(F32)).astype(BF16)
    return v1, v2, v3


def _adaln_kernel(c_ref, w_ref, b_ref, o_ref):
    s = _silu(c_ref[...]).astype(BF16)
    o_ref[0] = _dot(s, w_ref[0].astype(BF16)) + b_ref[0]


def _adaln(cond, ada_w, ada_b):
    depth, d, n = ada_w.shape
    rows = cond.shape[0]
    tn = 1024
    return pl.pallas_call(
        _adaln_kernel,
        grid=(depth, n // tn),
        in_specs=[pl.BlockSpec((rows, d), lambda l, j: (0, 0)),
                  pl.BlockSpec((1, d, tn), lambda l, j: (l, 0, j)),
                  pl.BlockSpec((1, 1, tn), lambda l, j: (l, 0, j))],
        out_specs=pl.BlockSpec((1, rows, tn), lambda l, j: (l, 0, j)),
        out_shape=jax.ShapeDtypeStruct((depth, rows, n), F32),
        compiler_params=_params("parallel", "arbitrary"),
        name="adaln",
    )(cond, ada_w, ada_b.reshape(depth, 1, n))


def _norm_modulate(x, nw, shift, scale):
    y = x * lax.rsqrt(jnp.mean(x * x, axis=-1, keepdims=True) + EPS)
    return ((y * nw) * (1.0 + scale) + shift).astype(BF16)


def _nmm_kernel(x_ref, nw_ref, sh_ref, sc_ref, w_ref, o_ref, h_ref):
    @pl.when(pl.program_id(2) == 0)
    def _():
        h_ref[...] = _norm_modulate(x_ref[0], nw_ref[...], sh_ref[0], sc_ref[0])

    o_ref[0] = _dot(h_ref[...], w_ref[...]).astype(o_ref.dtype)


def _norm_mod_matmul(x, nw, shift, scale, w, *, tm=512, tn=512, out_dtype=F32, name="nmm"):
    nb, l, d = x.shape
    n = w.shape[1]
    tm = min(tm, l)
    return pl.pallas_call(
        _nmm_kernel,
        grid=(nb, l // tm, n // tn),
        in_specs=[pl.BlockSpec((1, tm, d), lambda b, i, j: (b, i, 0)),
                  pl.BlockSpec((1, d), lambda b, i, j: (0, 0)),
                  pl.BlockSpec((1, 1, d), lambda b, i, j: (b, 0, 0)),
                  pl.BlockSpec((1, 1, d), lambda b, i, j: (b, 0, 0)),
                  pl.BlockSpec((d, tn), lambda b, i, j: (0, j))],
        out_specs=pl.BlockSpec((1, tm, tn), lambda b, i, j: (b, i, j)),
        out_shape=jax.ShapeDtypeStruct((nb, l, n), out_dtype),
        scratch_shapes=[pltpu.VMEM((tm, d), BF16)],
        compiler_params=_params("parallel", "parallel", "arbitrary"),
        name=name,
    )(x, nw.reshape(1, d), shift, scale, w)


_QKV_TN = 4 * HEAD_DIM
_N_Q_TILES = 4
_K_TILE = 4


def _qkv_kernel(x_ref, nw_ref, sh_ref, sc_ref, w_ref, hn_ref, cos_ref, sin_ref, o_ref, h_ref, *, rope):
    j = pl.program_id(2)

    @pl.when(j == 0)
    def _():
        h_ref[...] = _norm_modulate(x_ref[0], nw_ref[...], sh_ref[0], sc_ref[0])

    acc = _dot(h_ref[...], w_ref[...])

    @pl.when(j <= _K_TILE)
    def _():
        for hh in range(4):
            xh = acc[:, hh * HEAD_DIM:(hh + 1) * HEAD_DIM]
            xn = xh * lax.rsqrt(jnp.mean(xh * xh, axis=-1, keepdims=True) + EPS) * hn_ref[0]
            if rope:
                xn = xn * cos_ref[...] + pltpu.roll(xn, HEAD_DIM // 2, 1) * sin_ref[...]
            o_ref[0, :, hh * HEAD_DIM:(hh + 1) * HEAD_DIM] = xn.astype(o_ref.dtype)

    @pl.when(j > _K_TILE)
    def _():
        o_ref[0] = acc.astype(o_ref.dtype)


def _qkv_proj(x, nw, shift, scale, w, head_norms, cos_t, sin_t, *, rope, out_dtype, tm=512):
    nb, l, d = x.shape
    n = w.shape[1]
    tm = min(tm, l)
    tn = _QKV_TN
    return pl.pallas_call(
        functools.partial(_qkv_kernel, rope=rope),
        grid=(nb, l // tm, n // tn),
        in_specs=[pl.BlockSpec((1, tm, d), lambda b, i, j: (b, i, 0)),
                  pl.BlockSpec((1, d), lambda b, i, j: (0, 0)),
                  pl.BlockSpec((1, 1, d), lambda b, i, j: (b, 0, 0)),
                  pl.BlockSpec((1, 1, d), lambda b, i, j: (b, 0, 0)),
                  pl.BlockSpec((d, tn), lambda b, i, j: (0, j)),
                  pl.BlockSpec((1, 1, HEAD_DIM), lambda b, i, j: (jnp.minimum(j // _N_Q_TILES, 1), 0, 0)),
                  pl.BlockSpec((tm, HEAD_DIM), lambda b, i, j: (i, 0)),
                  pl.BlockSpec((tm, HEAD_DIM), lambda b, i, j: (i, 0))],
        out_specs=pl.BlockSpec((1, tm, tn), lambda b, i, j: (b, i, j)),
        out_shape=jax.ShapeDtypeStruct((nb, l, n), out_dtype),
        scratch_shapes=[pltpu.VMEM((tm, d), BF16)],
        compiler_params=_params("parallel", "parallel", "arbitrary"),
        name="qkv_proj",
    )(x, nw.reshape(1, d), shift, scale, w, head_norms, cos_t, sin_t)


def _attn_kernel(*refs, tk, has_ctx):
    if has_ctx:
        q_ref, k_ref, v_ref, kc_ref, vc_ref, o_ref = refs
    else:
        q_ref, k_ref, v_ref, o_ref = refs
    tq = q_ref.shape[1]
    lk = k_ref.shape[1]
    scale = HEAD_DIM ** -0.5
    q = q_ref[0]
    q4 = jnp.concatenate([q[:, g * HEAD_DIM:(g + 1) * HEAD_DIM] for g in range(Q_PER_KV)], axis=0).astype(BF16)
    rows = Q_PER_KV * tq

    def step(kc, vc, carry):
        m, l, acc = carry
        s = lax.dot_general(q4, kc, (((1,), (1,)), ((), ())), preferred_element_type=F32) * scale
        m_new = jnp.maximum(m, jnp.max(s, axis=-1, keepdims=True))
        alpha = jnp.exp(m - m_new)
        p = jnp.exp(s - m_new)
        l = alpha * l + jnp.sum(p, axis=-1, keepdims=True)
        acc = alpha * acc + _dot(p.astype(BF16), vc)
        return m_new, l, acc

    def body(ci, carry):
        start = pl.multiple_of(ci * tk, tk)
        kc = k_ref[0, pl.ds(start, tk), :].astype(BF16)
        vc = v_ref[0, pl.ds(start, tk), :].astype(BF16)
        return step(kc, vc, carry)

    carry = (jnp.full((rows, 1), -jnp.inf, F32), jnp.zeros((rows, 1), F32), jnp.zeros((rows, HEAD_DIM), F32))
    carry = lax.fori_loop(0, lk // tk, body, carry)
    if has_ctx:
        carry = step(kc_ref[0].astype(BF16), vc_ref[0].astype(BF16), carry)
    _, l, acc = carry
    o = acc / l
    for g in range(Q_PER_KV):
        o_ref[0, :, g * HEAD_DIM:(g + 1) * HEAD_DIM] = o[g * tq:(g + 1) * tq].astype(o_ref.dtype)


def _attention(qkv, k_ctx=None, v_ctx=None, *, tq=256, tk=512):
    b, l, _ = qkv.shape
    tq = min(tq, l)
    tk = min(tk, l)
    has_ctx = k_ctx is not None
    n_q_blocks = (N_KV_HEADS * Q_PER_KV * HEAD_DIM) // (Q_PER_KV * HEAD_DIM)
    k_col0 = N_KV_HEADS * Q_PER_KV
    v_col0 = k_col0 + N_KV_HEADS
    in_specs = [pl.BlockSpec((1, tq, Q_PER_KV * HEAD_DIM), lambda bi, h, qi: (bi, qi, h)),
                pl.BlockSpec((1, l, HEAD_DIM), lambda bi, h, qi: (bi, 0, k_col0 + h)),
                pl.BlockSpec((1, l, HEAD_DIM), lambda bi, h, qi: (bi, 0, v_col0 + h))]
    args = [qkv, qkv, qkv]
    if has_ctx:
        p = k_ctx.shape[1]
        in_specs += [pl.BlockSpec((1, p, HEAD_DIM), lambda bi, h, qi: (bi, 0, h)),
                     pl.BlockSpec((1, p, HEAD_DIM), lambda bi, h, qi: (bi, 0, h))]
        args += [k_ctx, v_ctx]
    assert n_q_blocks == N_KV_HEADS
    return pl.pallas_call(
        functools.partial(_attn_kernel, tk=tk, has_ctx=has_ctx),
        grid=(b, N_KV_HEADS, l // tq),
        in_specs=in_specs,
        out_specs=pl.BlockSpec((1, tq, Q_PER_KV * HEAD_DIM), lambda bi, h, qi: (bi, qi, h)),
        out_shape=jax.ShapeDtypeStruct((b, l, N_KV_HEADS * Q_PER_KV * HEAD_DIM), BF16),
        compiler_params=_params("parallel", "parallel", "arbitrary"),
        name="attention",
    )(*args)


def _attn_out_kernel(a_ref, w_ref, x_ref, g_ref, o_ref):
    o_ref[0] = x_ref[0] + g_ref[0] * _dot(a_ref[0], w_ref[...])


def _ssd_out_kernel(y_ref, z_ref, nw_ref, w_ref, x_ref, g_ref, o_ref, h_ref):
    @pl.when(pl.program_id(2) == 0)
    def _():
        t = y_ref[0] * _silu(z_ref[0])
        t = t * lax.rsqrt(jnp.mean(t * t, axis=-1, keepdims=True) + EPS)
        h_ref[...] = (t * nw_ref[...]).astype(BF16)

    o_ref[0] = x_ref[0] + g_ref[0] * _dot(h_ref[...], w_ref[...])


def _lru_out_kernel(y_ref, gb_ref, w_ref, x_ref, g_ref, o_ref, h_ref):
    @pl.when(pl.program_id(2) == 0)
    def _():
        h_ref[...] = (y_ref[0] * jax.nn.gelu(gb_ref[0], approximate=True)).astype(BF16)

    o_ref[0] = x_ref[0] + g_ref[0] * _dot(h_ref[...], w_ref[...])


def _mixer_out(kind, acts, w, x, gate, *, tm=512, tn=512):
    nb, l, d = x.shape
    k = w.shape[0]
    tm = min(tm, l)
    act_specs = [pl.BlockSpec((1, tm, k), lambda b, i, j: (b, i, 0)) for _ in acts]
    tail_specs = [pl.BlockSpec((k, tn), lambda b, i, j: (0, j)),
                  pl.BlockSpec((1, tm, tn), lambda b, i, j: (b, i, j)),
                  pl.BlockSpec((1, 1, tn), lambda b, i, j: (b, 0, j))]
    scratch = [pltpu.VMEM((tm, k), BF16)]
    if kind == "attn":
        body, extra, extra_specs, scratch = _attn_out_kernel, [], [], []
    elif kind == "ssd":
        body, extra = _ssd_out_kernel, [acts.pop()]
        act_specs.pop()
        extra_specs = [pl.BlockSpec((1, k), lambda b, i, j: (0, 0))]
    else:
        body, extra, extra_specs = _lru_out_kernel, [], []
    return pl.pallas_call(
        body,
        grid=(nb, l // tm, d // tn),
        in_specs=act_specs + extra_specs + tail_specs,
        out_specs=pl.BlockSpec((1, tm, tn), lambda b, i, j: (b, i, j)),
        out_shape=jax.ShapeDtypeStruct((nb, l, d), F32),
        scratch_shapes=scratch,
        compiler_params=_params("parallel", "parallel", "arbitrary"),
        name=kind + "_out",
    )(*acts, *extra, w, x, gate)


def _ffn_kernel(x_ref, nw_ref, sh_ref, sc_ref, g_ref, wg_ref, wu_ref, wo_ref, o_ref, h_ref, acc_ref):
    j = pl.program_id(2)

    @pl.when(j == 0)
    def _():
        h_ref[...] = _norm_modulate(x_ref[0], nw_ref[...], sh_ref[0], sc_ref[0])
        acc_ref[...] = jnp.zeros_like(acc_ref)

    h = h_ref[...]
    a = (_silu(_dot(h, wg_ref[...])) * _dot(h, wu_ref[...])).astype(BF16)
    acc_ref[...] += _dot(a, wo_ref[...])

    @pl.when(j == pl.num_programs(2) - 1)
    def _():
        o_ref[0] = x_ref[0] + g_ref[0] * acc_ref[...]


def _ffn(x, nw, shift, scale, gate, w_in, w_out, *, tm=512, tf=512):
    nb, l, d = x.shape
    f = w_out.shape[0]
    tm = min(tm, l)
    nf = f // tf
    return pl.pallas_call(
        _ffn_kernel,
        grid=(nb, l // tm, nf),
        in_specs=[pl.BlockSpec((1, tm, d), lambda b, i, j: (b, i, 0)),
                  pl.BlockSpec((1, d), lambda b, i, j: (0, 0)),
                  pl.BlockSpec((1, 1, d), lambda b, i, j: (b, 0, 0)),
                  pl.BlockSpec((1, 1, d), lambda b, i, j: (b, 0, 0)),
                  pl.BlockSpec((1, 1, d), lambda b, i, j: (b, 0, 0)),
                  pl.BlockSpec((d, tf), lambda b, i, j: (0, j)),
                  pl.BlockSpec((d, tf), lambda b, i, j: (0, nf + j)),
                  pl.BlockSpec((tf, d), lambda b, i, j: (j, 0))],
        out_specs=pl.BlockSpec((1, tm, d), lambda b, i, j: (b, i, 0)),
        out_shape=jax.ShapeDtypeStruct((nb, l, d), F32),
        scratch_shapes=[pltpu.VMEM((tm, d), BF16), pltpu.VMEM((tm, d), F32)],
        compiler_params=_params("parallel", "parallel", "arbitrary"),
        name="ffn",
    )(x, nw.reshape(1, d), shift, scale, gate, w_in, w_in, w_out)


def _conv_kernel(prev_ref, cur_ref, nxt_ref, w_ref, b_ref, o_ref, *, silu):
    i = pl.program_id(1)
    tl = cur_ref.shape[1]
    cur = cur_ref[0]
    prev = jnp.where(i > 0, prev_ref[0], 0.0)
    nxt = jnp.where(i < pl.num_programs(1) - 1, nxt_ref[0], 0.0)
    cat = jnp.concatenate([prev, cur, nxt], axis=0)
    w = w_ref[...]
    h = SUBLANES
    y = (cat[h - 1:h - 1 + tl] * w[0:1] + cur * w[1:2] + cat[h + 1:h + 1 + tl] * w[2:3]
         + cat[h + 2:h + 2 + tl] * w[3:4] + b_ref[...])
    o_ref[0] = _silu(y) if silu else y


def _dwconv(src, col0, width, w, bias, *, silu, tl=512, ct=512):
    nb, l, _ = src.shape
    tl = min(tl, l)
    c0 = col0 // ct
    rb = tl // SUBLANES
    last = l // SUBLANES - 1
    return pl.pallas_call(
        functools.partial(_conv_kernel, silu=silu),
        grid=(nb, l // tl, width // ct),
        in_specs=[pl.BlockSpec((1, SUBLANES, ct), lambda b, i, j: (b, jnp.maximum(i * rb - 1, 0), c0 + j)),
                  pl.BlockSpec((1, tl, ct), lambda b, i, j: (b, i, c0 + j)),
                  pl.BlockSpec((1, SUBLANES, ct), lambda b, i, j: (b, jnp.minimum((i + 1) * rb, last), c0 + j)),
                  pl.BlockSpec((4, ct), lambda b, i, j: (0, j)),
                  pl.BlockSpec((1, ct), lambda b, i, j: (0, j))],
        out_specs=pl.BlockSpec((1, tl, ct), lambda b, i, j: (b, i, j)),
        out_shape=jax.ShapeDtypeStruct((nb, l, width), F32),
        compiler_params=_params("parallel", "parallel", "parallel"),
        name="dwconv",
    )(src, src, src, w, bias.reshape(1, width))


def _ssd_dt_kernel(raw_ref, bias_ref, alog_ref, dt_ref, cs_ref):
    n = SSD_CHUNK
    dt = _softplus(raw_ref[0] + bias_ref[...])
    da = dt * (-jnp.exp(alog_ref[...]))
    ii = lax.broadcasted_iota(jnp.int32, (n, n), 0)
    jj = lax.broadcasted_iota(jnp.int32, (n, n), 1)
    lower = (ii >= jj).astype(BF16)
    upper = (ii <= jj).astype(BF16)
    parts = _split3(da)
    pre = sum(_dot(lower, p) for p in parts)
    suf = sum(_dot(upper, p) for p in parts)
    lane = lax.broadcasted_iota(jnp.int32, pre.shape, 1)
    dt_ref[0] = dt
    cs_ref[0] = jnp.where(lane < pre.shape[1] // 2, pre, suf)


def _ssd_dt(proj, col0, dt_bias, a_log):
    nb, l, _ = proj.shape
    nh2 = dt_bias.size
    out = jax.ShapeDtypeStruct((nb, l, nh2), F32)
    return pl.pallas_call(
        _ssd_dt_kernel,
        grid=(nb, l // SSD_CHUNK),
        in_specs=[pl.BlockSpec((1, SSD_CHUNK, nh2), lambda b, c: (b, c, col0 // nh2)),
                  pl.BlockSpec((1, nh2), lambda b, c: (0, 0)),
                  pl.BlockSpec((1, nh2), lambda b, c: (0, 0))],
        out_specs=[pl.BlockSpec((1, SSD_CHUNK, nh2), lambda b, c: (b, c, 0)),
                   pl.BlockSpec((1, SSD_CHUNK, nh2), lambda b, c: (b, c, 0))],
        out_shape=[out, out],
        compiler_params=_params("parallel", "parallel"),
        name="ssd_dt",
    )(proj, dt_bias.reshape(1, nh2), a_log.reshape(1, nh2))


_GROUP_W = 512
_HEADS_PER_GROUP = 8
_PAIR_W = 2 * SSD_HEAD_DIM


def _ssd_direction(d, g, x_ref, b_ref, c_ref, dt_ref, cs_ref, cst_ref, dsk_ref, st_ref):
    n = SSD_CHUNK
    x = x_ref[0]
    bm = b_ref[0]
    cm = c_ref[0].astype(BF16)
    n_heads = dt_ref.shape[2] // 2
    hbase = d * n_heads + g * _HEADS_PER_GROUP
    shift = (LANES - hbase) % LANES
    dtg = pltpu.roll(dt_ref[0], shift, 1)
    csg = pltpu.roll(cs_ref[0], shift, 1)
    bt = bm.T.astype(BF16)
    scores = _dot(cm, bt)
    ii = lax.broadcasted_iota(jnp.int32, (n, n), 0)
    jj = lax.broadcasted_iota(jnp.int32, (n, n), 1)
    mask = (ii >= jj) if d == 0 else (ii <= jj)
    low = jj < SSD_HEAD_DIM
    edge = n - 1 if d == 0 else 0
    st = st_ref[d]
    y_inter = _dot(cm, st.astype(BF16))
    slabs = []
    for p in range(_HEADS_PER_GROUP // 2):
        h0, h1 = 2 * p, 2 * p + 1
        sl = slice(p * _PAIR_W, (p + 1) * _PAIR_W)
        c0 = jnp.broadcast_to(csg[:, h0:h0 + 1], (n, n))
        c1 = jnp.broadcast_to(csg[:, h1:h1 + 1], (n, n))
        r0 = cst_ref[0, pl.ds(hbase + h0, 1), :]
        r1 = cst_ref[0, pl.ds(hbase + h1, 1), :]
        l0 = jnp.exp(jnp.where(mask, c0 - r0, -jnp.inf))
        l1 = jnp.exp(jnp.where(mask, c1 - r1, -jnp.inf))
        m = jnp.concatenate([scores * l0, scores * l1], axis=1).astype(BF16)
        dts = jnp.where(low, jnp.broadcast_to(dtg[:, h0:h0 + 1], (n, n)), jnp.broadcast_to(dtg[:, h1:h1 + 1], (n, n)))
        css = jnp.where(low, c0, c1)
        tot = css[edge:edge + 1, :]
        xs = x[:, sl]
        xdt = xs * dts
        rhs = jnp.concatenate([jnp.where(low, xdt, 0.0), jnp.where(low, 0.0, xdt)], axis=0).astype(BF16)
        y = _dot(m, rhs) + y_inter[:, sl] * jnp.exp(css)
        if d == 0:
            y = y + xs * dsk_ref[0][:, sl]
        slabs.append(y)
        wgt = (xdt * jnp.exp(tot - css)).astype(BF16)
        st_ref[d, :, sl] = st[:, sl] * jnp.exp(tot) + _dot(bt, wgt)
    return jnp.concatenate(slabs, axis=1)


def _ssd_kernel(xf_ref, bf_ref, cf_ref, dtf_ref, csf_ref, cstf_ref,
                xb_ref, bb_ref, cb_ref, dtb_ref, csb_ref, cstb_ref,
                dsk_ref, h0_ref, y_ref, ht_ref, st_ref):
    g = pl.program_id(1)
    c = pl.program_id(2)
    nc = pl.num_programs(2)

    @pl.when(c == 0)
    def _():
        st_ref[0] = h0_ref[0, 0, 0]
        st_ref[1] = h0_ref[0, 1, 0]

    yf = _ssd_direction(0, g, xf_ref, bf_ref, cf_ref, dtf_ref, csf_ref, cstf_ref, dsk_ref, st_ref)
    yb = _ssd_direction(1, g, xb_ref, bb_ref, cb_ref, dtb_ref, csb_ref, cstb_ref, dsk_ref, st_ref)
    rows_f = pl.ds(pl.multiple_of(c * SSD_CHUNK, SSD_CHUNK), SSD_CHUNK)
    rows_b = pl.ds(pl.multiple_of((nc - 1 - c) * SSD_CHUNK, SSD_CHUNK), SSD_CHUNK)

    @pl.when(c < nc // 2)
    def _():
        y_ref[0, rows_f, :] = yf
        y_ref[0, rows_b, :] = yb

    @pl.when(c >= nc // 2)
    def _():
        y_ref[0, rows_f, :] += yf
        y_ref[0, rows_b, :] += yb

    @pl.when(c == nc - 1)
    def _():
        ht_ref[0, 0, 0] = st_ref[0]
        ht_ref[0, 1, 0] = st_ref[1]


def _ssd_scan(xbc, dt, cs, cst, d_skip_row, h0):
    nb, l, _ = xbc.shape
    nc = l // SSD_CHUNK
    assert nc % 2 == 0
    n = SSD_CHUNK
    d_inner = SSD_GROUPS * _GROUP_W
    b_blk0 = d_inner // SSD_STATE
    c_blk0 = b_blk0 + SSD_GROUPS
    nh2 = dt.shape[2]

    def chunk_specs(which):
        return [pl.BlockSpec((1, n, _GROUP_W), lambda b, g, c: (b, which(c), g)),
                pl.BlockSpec((1, n, SSD_STATE), lambda b, g, c: (b, which(c), b_blk0 + g)),
                pl.BlockSpec((1, n, SSD_STATE), lambda b, g, c: (b, which(c), c_blk0 + g)),
                pl.BlockSpec((1, n, nh2), lambda b, g, c: (b, which(c), 0)),
                pl.BlockSpec((1, n, nh2), lambda b, g, c: (b, which(c), 0)),
                pl.BlockSpec((1, nh2, n), lambda b, g, c: (b, 0, which(c)))]

    state_shape = jax.ShapeDtypeStruct(h0.shape, F32)
    return pl.pallas_call(
        _ssd_kernel,
        grid=(nb, SSD_GROUPS, nc),
        in_specs=(chunk_specs(lambda c: c) + chunk_specs(lambda c: nc - 1 - c)
                  + [pl.BlockSpec((1, 1, _GROUP_W), lambda b, g, c: (g, 0, 0)),
                     pl.BlockSpec((1, 2, 1, SSD_STATE, _GROUP_W), lambda b, g, c: (b, 0, g, 0, 0))]),
        out_specs=[pl.BlockSpec((1, l, _GROUP_W), lambda b, g, c: (b, 0, g)),
                   pl.BlockSpec((1, 2, 1, SSD_STATE, _GROUP_W), lambda b, g, c: (b, 0, g, 0, 0))],
        out_shape=[jax.ShapeDtypeStruct((nb, l, d_inner), F32), state_shape],
        scratch_shapes=[pltpu.VMEM((2, SSD_STATE, _GROUP_W), F32)],
        compiler_params=_params("parallel", "parallel", "arbitrary"),
        name="ssd_scan",
    )(xbc, xbc, xbc, dt, cs, cst, xbc, xbc, xbc, dt, cs, cst, d_skip_row, h0)


_LRU_CT = 512
_LRU_ROWS = 128


def _tile_scan(a, u, h_in, reverse):
    rows = lax.broadcasted_iota(jnp.int32, a.shape, 0)
    for s in (1, 2, 4):
        if reverse:
            keep = rows < SUBLANES - s
            amount = SUBLANES - s
        else:
            keep = rows >= s
            amount = s
        a_s = jnp.where(keep, pltpu.roll(a, amount, 0), 1.0)
        u_s = jnp.where(keep, pltpu.roll(u, amount, 0), 0.0)
        u = a * u_s + u
        a = a * a_s
    h = a * h_in + u
    return h, (h[0:1] if reverse else h[SUBLANES - 1:SUBLANES])


def _lru_kernel(rec_ref, w_ref, ba_ref, bi_ref, ap_ref, h0_ref, o_ref, ht_ref, a_scr, u_scr):
    l = rec_ref.shape[1]
    nch = l // _LRU_ROWS
    n_tiles = _LRU_ROWS // SUBLANES
    n_blk = _LRU_CT // LRU_BLOCK_W

    def gates(d, start):
        rec = rec_ref[0, pl.ds(start, _LRU_ROWS), :]
        sp = _softplus(-ap_ref[d])
        for k in range(n_blk):
            sl = slice(k * LRU_BLOCK_W, (k + 1) * LRU_BLOCK_W)
            rk = rec[:, sl]
            pre = _dot(rk.astype(BF16), w_ref[d, k])
            r = _sigmoid(pre[:, :LRU_BLOCK_W] + ba_ref[d][:, sl])
            ig = _sigmoid(pre[:, LRU_BLOCK_W:] + bi_ref[d][:, sl])
            log_a = -LRU_C * r * sp[:, sl]
            a = jnp.exp(log_a)
            a_scr[d, :, sl] = a
            u_scr[d, :, sl] = jnp.sqrt(1.0 - a * a) * (ig * rk)

    def chunk(c, carry, accumulate):
        hf, hb = carry
        start_f = pl.multiple_of(c * _LRU_ROWS, _LRU_ROWS)
        start_b = pl.multiple_of((nch - 1 - c) * _LRU_ROWS, _LRU_ROWS)
        gates(0, start_f)
        gates(1, start_b)
        for t in range(n_tiles):
            tf = t * SUBLANES
            tb = (n_tiles - 1 - t) * SUBLANES
            hs_f, hf = _tile_scan(a_scr[0, tf:tf + SUBLANES, :], u_scr[0, tf:tf + SUBLANES, :], hf, False)
            hs_b, hb = _tile_scan(a_scr[1, tb:tb + SUBLANES, :], u_scr[1, tb:tb + SUBLANES, :], hb, True)
            rows_f = pl.ds(pl.multiple_of(start_f + tf, SUBLANES), SUBLANES)
            rows_b = pl.ds(pl.multiple_of(start_b + tb, SUBLANES), SUBLANES)
            if accumulate:
                o_ref[0, rows_f, :] += hs_f
                o_ref[0, rows_b, :] += hs_b
            else:
                o_ref[0, rows_f, :] = hs_f
                o_ref[0, rows_b, :] = hs_b
        return hf, hb

    carry = (h0_ref[0, 0], h0_ref[0, 1])
    carry = lax.fori_loop(0, nch // 2, lambda c, cr: chunk(c, cr, False), carry)
    hf, hb = lax.fori_loop(nch // 2, nch, lambda c, cr: chunk(c, cr, True), carry)
    ht_ref[0, 0] = hf
    ht_ref[0, 1] = hb


def _lru_scan(rec, w_gate, b_a, b_i, a_param, h0):
    nb, l, w = rec.shape
    assert (l // _LRU_ROWS) % 2 == 0
    n_blk = _LRU_CT // LRU_BLOCK_W
    vec = pl.BlockSpec((2, 1, _LRU_CT), lambda b, j: (0, 0, j))
    return pl.pallas_call(
        _lru_kernel,
        grid=(nb, w // _LRU_CT),
        in_specs=[pl.BlockSpec((1, l, _LRU_CT), lambda b, j: (b, 0, j)),
                  pl.BlockSpec((2, n_blk, LRU_BLOCK_W, 2 * LRU_BLOCK_W), lambda b, j: (0, j, 0, 0)),
                  vec, vec, vec,
                  pl.BlockSpec((1, 2, 1, _LRU_CT), lambda b, j: (b, 0, 0, j))],
        out_specs=[pl.BlockSpec((1, l, _LRU_CT), lambda b, j: (b, 0, j)),
                   pl.BlockSpec((1, 2, 1, _LRU_CT), lambda b, j: (b, 0, 0, j))],
        out_shape=[jax.ShapeDtypeStruct((nb, l, w), F32), jax.ShapeDtypeStruct((nb, 2, 1, w), F32)],
        scratch_shapes=[pltpu.VMEM((2, _LRU_ROWS, _LRU_CT), F32), pltpu.VMEM((2, _LRU_ROWS, _LRU_CT), F32)],
        compiler_params=_params("parallel", "parallel"),
        name="lru_scan",
    )(rec, w_gate, b_a, b_i, a_param, h0)


def _final_norm_kernel(x_ref, w_ref, o_ref):
    x = x_ref[0]
    o_ref[0] = x * lax.rsqrt(jnp.mean(x * x, axis=-1, keepdims=True) + EPS) * w_ref[...]


def _final_norm(x, w, *, tm=512):
    nb, l, d = x.shape
    tm = min(tm, l)
    return pl.pallas_call(
        _final_norm_kernel,
        grid=(nb, l // tm),
        in_specs=[pl.BlockSpec((1, tm, d), lambda b, i: (b, i, 0)), pl.BlockSpec((1, d), lambda b, i: (0, 0))],
        out_specs=pl.BlockSpec((1, tm, d), lambda b, i: (b, i, 0)),
        out_shape=jax.ShapeDtypeStruct((nb, l, d), F32),
        compiler_params=_params("parallel", "parallel"),
        name="final_norm",
    )(x, w.reshape(1, d))


def _rope_tables(length):
    pos = jnp.arange(length)
    row = (pos // GRID_W).astype(F32)
    col = (pos % GRID_W).astype(F32)
    n_freq = HEAD_DIM // 4
    inv = ROPE_THETA ** (-jnp.arange(n_freq, dtype=F32) / n_freq)
    ang = jnp.concatenate([row[:, None] * inv, col[:, None] * inv], axis=-1)
    cos, sin = jnp.cos(ang), jnp.sin(ang)
    return jnp.concatenate([cos, cos], axis=-1), jnp.concatenate([-sin, sin], axis=-1)


def _attn_mixer(x, seq_shape, mods, nw, w_qkv, head_norms, w_o, rope, k_ctx, v_ctx):
    shift, scale, gate = mods
    nb, l, d = x.shape
    b, s = seq_shape
    use_rope = rope is not None
    cos_t, sin_t = rope if use_rope else (jnp.zeros((l, HEAD_DIM), F32), jnp.zeros((l, HEAD_DIM), F32))
    qkv = _qkv_proj(x, nw, shift, scale, w_qkv, head_norms, cos_t, sin_t, rope=use_rope,
                    out_dtype=BF16 if use_rope else F32)
    qkv_seq = qkv.reshape(b, s, qkv.shape[-1])
    o = _attention(qkv_seq, k_ctx, v_ctx)
    x = _mixer_out("attn", [o.reshape(nb, l, d)], w_o, x, gate)
    return x, qkv_seq


def _ssd_mixer(x, seq_shape, mods, nw, w_in, conv_w, conv_b, dt_bias, a_log, d_skip_row, norm_w, w_out, h0):
    shift, scale, gate = mods
    nb, l, d = x.shape
    b, s = seq_shape
    d_inner = w_out.shape[0]
    conv_dim = conv_w.shape[1]
    proj = _norm_mod_matmul(x, nw, shift, scale, w_in, name="ssd_in")
    proj_seq = proj.reshape(b, s, proj.shape[-1])
    xbc = _dwconv(proj_seq, d_inner, conv_dim, conv_w, conv_b, silu=True)
    dt, cs = _ssd_dt(proj_seq, d_inner + conv_dim, dt_bias, a_log)
    y, h_t = _ssd_scan(xbc, dt, cs, jnp.swapaxes(cs, 1, 2), d_skip_row, h0)
    x = _mixer_out("ssd", [y.reshape(nb, l, d_inner), proj, norm_w.reshape(1, d_inner)], w_out, x, gate, tm=256)
    return x, h_t


def _lru_mixer(x, seq_shape, mods, nw, w_in, conv_w, conv_b, w_gate, b_a, b_i, a_param, w_out, h0):
    shift, scale, gate = mods
    nb, l, d = x.shape
    b, s = seq_shape
    width = w_out.shape[0]
    proj = _norm_mod_matmul(x, nw, shift, scale, w_in, name="lru_in")
    rec = _dwconv(proj.reshape(b, s, 2 * width), width, width, conv_w, conv_b, silu=False)
    hs, h_t = _lru_scan(rec, w_gate, b_a, b_i, a_param, h0)
    x = _mixer_out("lru", [hs.reshape(nb, l, width), proj], w_out, x, gate)
    return x, h_t


def _states_to_kernel_layout(h):
    b, nh, p, n = h.shape
    return h.reshape(b, SSD_GROUPS, nh // SSD_GROUPS, p, n).transpose(0, 1, 4, 2, 3).reshape(b, SSD_GROUPS, n, -1)


def _states_from_kernel_layout(h, nh, p):
    b, two, g, n, _ = h.shape
    return h.reshape(b, two, g, n, nh // g, p).transpose(0, 1, 2, 4, 5, 3).reshape(b, two, nh, p, n)


def kernel(x_prompt, x_sample, cache_k, cache_v, state_ssm, state_lru, c, c_ctx, ada_w, ada_b, norm_mix, norm_ffn, attn_w_qkv, attn_q_norm, attn_k_norm, attn_w_o, ssd_w_in, ssd_conv_w, ssd_conv_b, ssd_dt_bias, ssd_a_log, ssd_d, ssd_norm, ssd_w_out, lru_w_in, lru_conv_w, lru_conv_b, lru_w_a, lru_b_a, lru_w_i, lru_b_i, lru_a_param, lru_w_out, ffn_w_in, ffn_w_out, final_norm):
    bp, sp, d = x_prompt.shape
    bs, ss, _ = x_sample.shape
    depth = ada_w.shape[0]
    n_mixers = 3

    pad_rows = -(bs + 1) % SUBLANES
    cond = jnp.concatenate([c, c_ctx[None, :], jnp.zeros((pad_rows, d), F32)], axis=0)
    mod = _adaln(cond, ada_w, ada_b).reshape(depth, bs + 1 + pad_rows, 6, 1, d)

    xp = x_prompt.reshape(1, bp * sp, d)
    xs = x_sample
    streams = {"p": (bp, sp), "s": (bs, ss)}
    rope = _rope_tables(ss)

    ssd_nh = ssd_d.shape[1]
    ssd_p = state_ssm.shape[4]
    new_k, new_v, new_ssm, new_lru = [], [], [], []
    for i in range(depth):
        kind, j = i % n_mixers, i // n_mixers
        mp = [mod[i, bs:bs + 1, t] for t in range(6)]
        ms = [mod[i, :bs, t] for t in range(6)]
        if kind == 0:
            w_qkv = attn_w_qkv[j].astype(BF16)
            w_o = attn_w_o[j].astype(BF16)
            head_norms = jnp.stack([attn_q_norm[j], attn_k_norm[j]]).reshape(2, 1, HEAD_DIM)
            kv_w = N_KV_HEADS * HEAD_DIM
            xp, qkv_p = _attn_mixer(xp, streams["p"], mp[:3], norm_mix[i], w_qkv, head_norms, w_o, None, None, None)
            xs, _ = _attn_mixer(xs, streams["s"], ms[:3], norm_mix[i], w_qkv, head_norms, w_o, rope,
                                cache_k[:, j].reshape(bs, -1, kv_w), cache_v[:, j].reshape(bs, -1, kv_w))
            q_w = qkv_p.shape[-1] - 2 * kv_w
            new_k.append(qkv_p[:, :, q_w:q_w + kv_w].reshape(bp, sp, N_KV_HEADS, HEAD_DIM))
            new_v.append(qkv_p[:, :, q_w + kv_w:].reshape(bp, sp, N_KV_HEADS, HEAD_DIM))
        elif kind == 1:
            w_in = ssd_w_in[j].astype(BF16)
            w_in = jnp.pad(w_in, ((0, 0), (0, -w_in.shape[1] % 512)))
            w_out = ssd_w_out[j].astype(BF16)
            d_skip_row = jnp.repeat(ssd_d[j], ssd_p).reshape(SSD_GROUPS, 1, _GROUP_W)
            args = (norm_mix[i], w_in, ssd_conv_w[j], ssd_conv_b[j], ssd_dt_bias[j], ssd_a_log[j], d_skip_row,
                    ssd_norm[j], w_out)
            zero = jnp.zeros((bp, 2, SSD_GROUPS, SSD_STATE, _GROUP_W), F32)
            h0 = jnp.stack([_states_to_kernel_layout(state_ssm[:, j, 0]),
                            _states_to_kernel_layout(state_ssm[:, j, 1])], axis=1)
            xp, h_t = _ssd_mixer(xp, streams["p"], mp[:3], *args, zero)
            xs, _ = _ssd_mixer(xs, streams["s"], ms[:3], *args, h0)
            new_ssm.append(_states_from_kernel_layout(h_t, ssd_nh, ssd_p))
        else:
            w_in = lru_w_in[j].astype(BF16)
            w_out = lru_w_out[j].astype(BF16)
            width = w_out.shape[0]
            w_gate = jnp.concatenate([lru_w_a[j], lru_w_i[j]], axis=-1).astype(BF16)
            args = (norm_mix[i], w_in, lru_conv_w[j], lru_conv_b[j], w_gate, lru_b_a[j].reshape(2, 1, width),
                    lru_b_i[j].reshape(2, 1, width), lru_a_param[j].reshape(2, 1, width), w_out)
            xp, h_t = _lru_mixer(xp, streams["p"], mp[:3], *args, jnp.zeros((bp, 2, 1, width), F32))
            xs, _ = _lru_mixer(xs, streams["s"], ms[:3], *args, state_lru[:, j].reshape(bs, 2, 1, width))
            new_lru.append(h_t.reshape(bp, 2, width))
        w1 = ffn_w_in[i].astype(BF16)
        w2 = ffn_w_out[i].astype(BF16)
        xp = _ffn(xp, norm_ffn[i], mp[3], mp[4], mp[5], w1, w2)
        xs = _ffn(xs, norm_ffn[i], ms[3], ms[4], ms[5], w1, w2)
    y_prompt = _final_norm(xp, final_norm).reshape(bp, sp, d)
    y_sample = _final_norm(xs, final_norm)
    return (y_prompt, y_sample, jnp.stack(new_k, axis=1), jnp.stack(new_v, axis=1),
            jnp.stack(new_ssm, axis=1), jnp.stack(new_lru, axis=1))
```

```python
import functools
import math

import jax
import jax.numpy as jnp
from jax import lax
from jax.experimental import pallas as pl
from jax.experimental.pallas import tpu as pltpu

F32 = jnp.float32
BF16 = jnp.bfloat16
EPS = 1e-6

GRID_W = 64
HEAD_DIM = 128
N_KV_HEADS = 4
Q_PER_KV = 4
ROPE_THETA = 10000.0
SSD_HEAD_DIM = 64
SSD_GROUPS = 8
SSD_STATE = 128
SSD_CHUNK = 128
LRU_BLOCK_W = 128
LRU_C = 8.0

LANES = 128
SUBLANES = 8
VMEM_LIMIT_BYTES = 56 * 1024 * 1024
VMEM_LIMIT_FFN_BYTES = 58 * 1024 * 1024


def _params(*sem, vmem=VMEM_LIMIT_BYTES):
    return pltpu.CompilerParams(dimension_semantics=sem, vmem_limit_bytes=vmem)


def _sigmoid(x):
    return 1.0 / (1.0 + jnp.exp(-x))


def _silu(x):
    return x * _sigmoid(x)


def _softplus(x):
    return jnp.maximum(x, 0.0) + jnp.log(1.0 + jnp.exp(-jnp.abs(x)))


def _dot(a, b):
    return jnp.dot(a, b, preferred_element_type=F32)


def _split3(v):
    v1 = v.astype(BF16)
    r1 = v - v1.astype(F32)
    v2 = r1.astype(BF16)
    v3 = (r1 - v2.astype(F32)).astype(BF16)
    return v1, v2, v3


def _adaln_kernel(c_ref, w_ref, b_ref, o_ref):
    s = _silu(c_ref[...]).astype(BF16)
    o_ref[0] = _dot(s, w_ref[0].astype(BF16)) + b_ref[0]


def _adaln(cond, ada_w, ada_b):
    depth, d, n = ada_w.shape
    rows = cond.shape[0]
    tn = 1024
    return pl.pallas_call(
        _adaln_kernel,
        grid=(depth, n // tn),
        in_specs=[pl.BlockSpec((rows, d), lambda l, j: (0, 0)),
                  pl.BlockSpec((1, d, tn), lambda l, j: (l, 0, j)),
                  pl.BlockSpec((1, 1, tn), lambda l, j: (l, 0, j))],
        out_specs=pl.BlockSpec((1, rows, tn), lambda l, j: (l, 0, j)),
        out_shape=jax.ShapeDtypeStruct((depth, rows, n), F32),
        compiler_params=_params("parallel", "arbitrary"),
        name="adaln",
    )(cond, ada_w, ada_b.reshape(depth, 1, n))


def _norm_modulate(x, nw, shift, scale):
    y = x * lax.rsqrt(jnp.mean(x * x, axis=-1, keepdims=True) + EPS)
    return ((y * nw) * (1.0 + scale) + shift).astype(BF16)


_ROW_CHUNK = 256


def _row_chunks(tm):
    rc = min(_ROW_CHUNK, tm)
    return [slice(r, r + rc) for r in range(0, tm, rc)]


def _nmm_kernel(x_ref, nw_ref, sh_ref, sc_ref, w_ref, o_ref, h_ref):
    j = pl.program_id(2)

    @pl.when(j == 0)
    def _():
        for rows in _row_chunks(h_ref.shape[0]):
            h = _norm_modulate(x_ref[0, rows, :], nw_ref[...], sh_ref[0], sc_ref[0])
            h_ref[rows, :] = h
            o_ref[0, rows, :] = _dot(h, w_ref[...]).astype(o_ref.dtype)

    @pl.when(j > 0)
    def _():
        o_ref[0] = _dot(h_ref[...], w_ref[...]).astype(o_ref.dtype)


def _norm_mod_matmul(x, nw, shift, scale, w, *, tm=1024, tn=512, out_dtype=F32, name="nmm"):
    nb, l, d = x.shape
    n = w.shape[1]
    tm = min(tm, l)
    return pl.pallas_call(
        _nmm_kernel,
        grid=(nb, l // tm, n // tn),
        in_specs=[pl.BlockSpec((1, tm, d), lambda b, i, j: (b, i, 0)),
                  pl.BlockSpec((1, d), lambda b, i, j: (0, 0)),
                  pl.BlockSpec((1, 1, d), lambda b, i, j: (b, 0, 0)),
                  pl.BlockSpec((1, 1, d), lambda b, i, j: (b, 0, 0)),
                  pl.BlockSpec((d, tn), lambda b, i, j: (0, j))],
        out_specs=pl.BlockSpec((1, tm, tn), lambda b, i, j: (b, i, j)),
        out_shape=jax.ShapeDtypeStruct((nb, l, n), out_dtype),
        scratch_shapes=[pltpu.VMEM((tm, d), BF16)],
        compiler_params=_params("parallel", "parallel", "arbitrary"),
        name=name,
    )(x, nw.reshape(1, d), shift, scale, w)


_QKV_TN = 4 * HEAD_DIM
_N_Q_TILES = 4
_K_TILE = 4


def _qkv_kernel(x_ref, nw_ref, sh_ref, sc_ref, w_ref, hn_ref, cos_ref, sin_ref, o_ref, h_ref, *, rope):
    j = pl.program_id(2)
    is_value_tile = j > _K_TILE

    def emit(rows, h):
        acc = _dot(h, w_ref[...])
        for hh in range(4):
            cols = slice(hh * HEAD_DIM, (hh + 1) * HEAD_DIM)
            xh = acc[:, cols]
            xn = xh * lax.rsqrt(jnp.mean(xh * xh, axis=-1, keepdims=True) + EPS) * hn_ref[0]
            if rope:
                xn = xn * cos_ref[rows, :] + pltpu.roll(xn, HEAD_DIM // 2, 1) * sin_ref[rows, :]
            o_ref[0, rows, cols] = jnp.where(is_value_tile, xh, xn).astype(o_ref.dtype)

    @pl.when(j == 0)
    def _():
        for rows in _row_chunks(h_ref.shape[0]):
            h = _norm_modulate(x_ref[0, rows, :], nw_ref[...], sh_ref[0], sc_ref[0])
            h_ref[rows, :] = h
            emit(rows, h)

    @pl.when(j > 0)
    def _():
        for rows in _row_chunks(h_ref.shape[0]):
            emit(rows, h_ref[rows, :])


def _qkv_proj(x, nw, shift, scale, w, head_norms, cos_t, sin_t, *, rope, out_dtype, tm=1024):
    nb, l, d = x.shape
    n = w.shape[1]
    tm = min(tm, l)
    tn = _QKV_TN
    return pl.pallas_call(
        functools.partial(_qkv_kernel, rope=rope),
        grid=(nb, l // tm, n // tn),
        in_specs=[pl.BlockSpec((1, tm, d), lambda b, i, j: (b, i, 0)),
                  pl.BlockSpec((1, d), lambda b, i, j: (0, 0)),
                  pl.BlockSpec((1, 1, d), lambda b, i, j: (b, 0, 0)),
                  pl.BlockSpec((1, 1, d), lambda b, i, j: (b, 0, 0)),
                  pl.BlockSpec((d, tn), lambda b, i, j: (0, j)),
                  pl.BlockSpec((1, 1, HEAD_DIM), lambda b, i, j: (jnp.minimum(j // _N_Q_TILES, 1), 0, 0)),
                  pl.BlockSpec((tm, HEAD_DIM), lambda b, i, j: (i, 0)),
                  pl.BlockSpec((tm, HEAD_DIM), lambda b, i, j: (i, 0))],
        out_specs=pl.BlockSpec((1, tm, tn), lambda b, i, j: (b, i, j)),
        out_shape=jax.ShapeDtypeStruct((nb, l, n), out_dtype),
        scratch_shapes=[pltpu.VMEM((tm, d), BF16)],
        compiler_params=_params("parallel", "parallel", "arbitrary"),
        name="qkv_proj",
    )(x, nw.reshape(1, d), shift, scale, w, head_norms, cos_t, sin_t)


_ATTN_KEY_BLOCK = 256


def _attn_kernel(q_ref, k_ref, vt_ref, o_ref):
    tq = q_ref.shape[1]
    cols = Q_PER_KV * tq
    n_blocks, _, tkb = vt_ref.shape[2:]
    q = q_ref[0].astype(F32) * (HEAD_DIM ** -0.5 * math.log2(math.e))
    q4 = jnp.concatenate([q[:, g * HEAD_DIM:(g + 1) * HEAD_DIM] for g in range(Q_PER_KV)], axis=0)
    qt = q4.T.astype(BF16)

    def scores(kb):
        return _dot(k_ref[0, 0, pl.ds(pl.multiple_of(kb * tkb, tkb), tkb), :], qt)

    def update(kb, s, m, l, acc):
        m_new = jnp.maximum(m, jnp.max(s, axis=0, keepdims=True))
        alpha = jnp.exp2(m - m_new)
        p = jnp.exp2(s - m_new)
        l = alpha * l + jnp.sum(p, axis=0, keepdims=True)
        acc = alpha * acc + _dot(vt_ref[0, 0, kb], p.astype(BF16))
        return m_new, l, acc

    unroll = next(u for u in (9, 6, 3, 2, 1) if n_blocks % u == 0)

    def body(i, carry):
        ss = [scores(i * unroll + u) for u in range(unroll)]
        for u in range(unroll):
            carry = update(i * unroll + u, ss[u], *carry)
        return carry

    init = (jnp.full((1, cols), -jnp.inf, F32), jnp.zeros((1, cols), F32), jnp.zeros((HEAD_DIM, cols), F32))
    _, l, acc = lax.fori_loop(0, n_blocks // unroll, body, init)
    o = (acc / l).T
    for g in range(Q_PER_KV):
        o_ref[0, :, g * HEAD_DIM:(g + 1) * HEAD_DIM] = o[g * tq:(g + 1) * tq].astype(o_ref.dtype)


def _attention(qkv, k_ctx=None, v_ctx=None, *, tq=256):
    b, l, _ = qkv.shape
    tq = min(tq, l)
    q_w = N_KV_HEADS * Q_PER_KV * HEAD_DIM
    kv_w = N_KV_HEADS * HEAD_DIM
    k = qkv[:, :, q_w:q_w + kv_w].astype(BF16).reshape(b, l, N_KV_HEADS, HEAD_DIM)
    v = qkv[:, :, q_w + kv_w:].astype(BF16).reshape(b, l, N_KV_HEADS, HEAD_DIM)
    if k_ctx is not None:
        k = jnp.concatenate([k, k_ctx.astype(BF16)], axis=1)
        v = jnp.concatenate([v, v_ctx.astype(BF16)], axis=1)
    lk = k.shape[1]
    tkb = _ATTN_KEY_BLOCK
    assert lk % tkb == 0, (lk, tkb)
    n_blocks = lk // tkb
    k = k.transpose(0, 2, 1, 3)
    vt = v.reshape(b, n_blocks, tkb, N_KV_HEADS, HEAD_DIM).transpose(0, 3, 1, 4, 2)
    return pl.pallas_call(
        _attn_kernel,
        grid=(b, N_KV_HEADS, l // tq),
        in_specs=[pl.BlockSpec((1, tq, Q_PER_KV * HEAD_DIM), lambda bi, h, qi: (bi, qi, h)),
                  pl.BlockSpec((1, 1, lk, HEAD_DIM), lambda bi, h, qi: (bi, h, 0, 0)),
                  pl.BlockSpec((1, 1, n_blocks, HEAD_DIM, tkb), lambda bi, h, qi: (bi, h, 0, 0, 0))],
        out_specs=pl.BlockSpec((1, tq, Q_PER_KV * HEAD_DIM), lambda bi, h, qi: (bi, qi, h)),
        out_shape=jax.ShapeDtypeStruct((b, l, q_w), BF16),
        compiler_params=_params("parallel", "parallel", "arbitrary"),
        name="attention",
    )(qkv, k, vt)


def _plain_out_kernel(a_ref, w_ref, x_ref, g_ref, o_ref):
    o_ref[0] = x_ref[0] + g_ref[0] * _dot(a_ref[0], w_ref[...])


def _lru_out_kernel(y_ref, gb_ref, w_ref, x_ref, g_ref, o_ref, h_ref):
    j = pl.program_id(2)

    @pl.when(j == 0)
    def _():
        for rows in _row_chunks(h_ref.shape[0]):
            h = (y_ref[0, rows, :] * jax.nn.gelu(gb_ref[0, rows, :], approximate=True)).astype(BF16)
            h_ref[rows, :] = h
            o_ref[0, rows, :] = x_ref[0, rows, :] + g_ref[0] * _dot(h, w_ref[...])

    @pl.when(j > 0)
    def _():
        o_ref[0] = x_ref[0] + g_ref[0] * _dot(h_ref[...], w_ref[...])


def _mixer_out(kind, acts, w, x, gate, *, tm=1024, tn=512):
    nb, l, d = x.shape
    k = w.shape[0]
    tm = min(tm, l)
    act_specs = [pl.BlockSpec((1, tm, k), lambda b, i, j: (b, i, 0)) for _ in acts]
    tail_specs = [pl.BlockSpec((k, tn), lambda b, i, j: (0, j)),
                  pl.BlockSpec((1, tm, tn), lambda b, i, j: (b, i, j)),
                  pl.BlockSpec((1, 1, tn), lambda b, i, j: (b, 0, j))]
    if kind == "lru":
        body, scratch = _lru_out_kernel, [pltpu.VMEM((tm, k), BF16)]
    else:
        body, scratch = _plain_out_kernel, []
    return pl.pallas_call(
        body,
        grid=(nb, l // tm, d // tn),
        in_specs=act_specs + tail_specs,
        out_specs=pl.BlockSpec((1, tm, tn), lambda b, i, j: (b, i, j)),
        out_shape=jax.ShapeDtypeStruct((nb, l, d), F32),
        scratch_shapes=scratch,
        compiler_params=_params("parallel", "parallel", "arbitrary"),
        name=kind + "_out",
    )(*acts, w, x, gate)


def _gated_norm_kernel(y_ref, z_ref, nw_ref, o_ref):
    t = y_ref[0] * _silu(z_ref[0])
    t = t * lax.rsqrt(jnp.mean(t * t, axis=-1, keepdims=True) + EPS)
    o_ref[0] = (t * nw_ref[...]).astype(o_ref.dtype)


def _gated_norm(y, z_src, nw, *, tm=512):
    nb, l, k = y.shape
    tm = min(tm, l)
    return pl.pallas_call(
        _gated_norm_kernel,
        grid=(nb, l // tm),
        in_specs=[pl.BlockSpec((1, tm, k), lambda b, i: (b, i, 0)),
                  pl.BlockSpec((1, tm, k), lambda b, i: (b, i, 0)),
                  pl.BlockSpec((1, k), lambda b, i: (0, 0))],
        out_specs=pl.BlockSpec((1, tm, k), lambda b, i: (b, i, 0)),
        out_shape=jax.ShapeDtypeStruct((nb, l, k), BF16),
        compiler_params=_params("parallel", "parallel"),
        name="gated_norm",
    )(y, z_src, nw.reshape(1, k))


def _ffn_kernel(x_ref, nw_ref, sh_ref, sc_ref, g_ref, wg_ref, wu_ref, wo_ref, o_ref, h_ref):
    j = pl.program_id(2)

    def partial_out(h):
        a = (_silu(_dot(h, wg_ref[...])) * _dot(h, wu_ref[...])).astype(BF16)
        return _dot(a, wo_ref[...])

    @pl.when(j == 0)
    def _():
        for rows in _row_chunks(h_ref.shape[0]):
            h = _norm_modulate(x_ref[0, rows, :], nw_ref[...], sh_ref[0], sc_ref[0])
            h_ref[rows, :] = h
            o_ref[0, rows, :] = partial_out(h)

    @pl.when(j > 0)
    def _():
        o_ref[0] += partial_out(h_ref[...])

    @pl.when(j == pl.num_programs(2) - 1)
    def _():
        o_ref[0] = x_ref[0] + g_ref[0] * o_ref[0]


def _ffn(x, nw, shift, scale, gate, w_in, w_out, *, tm=1024, tf=512):
    nb, l, d = x.shape
    f = w_out.shape[0]
    tm = min(tm, l)
    nf = f // tf
    return pl.pallas_call(
        _ffn_kernel,
        grid=(nb, l // tm, nf),
        in_specs=[pl.BlockSpec((1, tm, d), lambda b, i, j: (b, i, 0), pipeline_mode=pl.Buffered(1)),
                  pl.BlockSpec((1, d), lambda b, i, j: (0, 0)),
                  pl.BlockSpec((1, 1, d), lambda b, i, j: (b, 0, 0)),
                  pl.BlockSpec((1, 1, d), lambda b, i, j: (b, 0, 0)),
                  pl.BlockSpec((1, 1, d), lambda b, i, j: (b, 0, 0)),
                  pl.BlockSpec((d, tf), lambda b, i, j: (0, j)),
                  pl.BlockSpec((d, tf), lambda b, i, j: (0, nf + j)),
                  pl.BlockSpec((tf, d), lambda b, i, j: (j, 0))],
        out_specs=pl.BlockSpec((1, tm, d), lambda b, i, j: (b, i, 0)),
        out_shape=jax.ShapeDtypeStruct((nb, l, d), F32),
        scratch_shapes=[pltpu.VMEM((tm, d), BF16)],
        compiler_params=_params("parallel", "parallel", "arbitrary", vmem=VMEM_LIMIT_FFN_BYTES),
        name="ffn",
    )(x, nw.reshape(1, d), shift, scale, gate, w_in, w_in, w_out)


def _conv_kernel(prev_ref, cur_ref, nxt_ref, w_ref, b_ref, o_ref, buf_ref, *, silu):
    i = pl.program_id(1)
    tl = cur_ref.shape[1]
    h = SUBLANES
    cur = cur_ref[0]
    buf_ref[0:h] = jnp.where(i > 0, prev_ref[0], 0.0)
    buf_ref[h:h + tl] = cur
    buf_ref[h + tl:h + tl + h] = jnp.where(i < pl.num_programs(1) - 1, nxt_ref[0], 0.0)
    w = w_ref[...]
    y = (buf_ref[h - 1:h - 1 + tl] * w[0:1] + cur * w[1:2] + buf_ref[h + 1:h + 1 + tl] * w[2:3]
         + buf_ref[h + 2:h + 2 + tl] * w[3:4] + b_ref[...])
    o_ref[0] = _silu(y) if silu else y


def _dwconv(src, col0, width, w, bias, *, silu, tl=512, ct=512):
    nb, l, _ = src.shape
    tl = min(tl, l)
    c0 = col0 // ct
    rb = tl // SUBLANES
    last = l // SUBLANES - 1
    return pl.pallas_call(
        functools.partial(_conv_kernel, silu=silu),
        grid=(nb, l // tl, width // ct),
        in_specs=[pl.BlockSpec((1, SUBLANES, ct), lambda b, i, j: (b, jnp.maximum(i * rb - 1, 0), c0 + j)),
                  pl.BlockSpec((1, tl, ct), lambda b, i, j: (b, i, c0 + j)),
                  pl.BlockSpec((1, SUBLANES, ct), lambda b, i, j: (b, jnp.minimum((i + 1) * rb, last), c0 + j)),
                  pl.BlockSpec((4, ct), lambda b, i, j: (0, j)),
                  pl.BlockSpec((1, ct), lambda b, i, j: (0, j))],
        out_specs=pl.BlockSpec((1, tl, ct), lambda b, i, j: (b, i, j)),
        out_shape=jax.ShapeDtypeStruct((nb, l, width), F32),
        scratch_shapes=[pltpu.VMEM((tl + 2 * SUBLANES, ct), F32)],
        compiler_params=_params("parallel", "parallel", "parallel"),
        name="dwconv",
    )(src, src, src, w, bias.reshape(1, width))


def _ssd_dt_kernel(raw_ref, bias_ref, alog_ref, dt_ref, cs_ref):
    n = SSD_CHUNK
    dt = _softplus(raw_ref[0] + bias_ref[...])
    da = dt * (-jnp.exp(alog_ref[...]))
    ii = lax.broadcasted_iota(jnp.int32, (n, n), 0)
    jj = lax.broadcasted_iota(jnp.int32, (n, n), 1)
    lower = (ii >= jj).astype(BF16)
    upper = (ii <= jj).astype(BF16)
    parts = _split3(da)
    pre = sum(_dot(lower, p) for p in parts)
    suf = sum(_dot(upper, p) for p in parts)
    lane = lax.broadcasted_iota(jnp.int32, pre.shape, 1)
    dt_ref[0] = dt
    cs_ref[0] = jnp.where(lane < pre.shape[1] // 2, pre, suf)


def _ssd_dt(proj, col0, dt_bias, a_log):
    nb, l, _ = proj.shape
    nh2 = dt_bias.size
    out = jax.ShapeDtypeStruct((nb, l, nh2), F32)
    return pl.pallas_call(
        _ssd_dt_kernel,
        grid=(nb, l // SSD_CHUNK),
        in_specs=[pl.BlockSpec((1, SSD_CHUNK, nh2), lambda b, c: (b, c, col0 // nh2)),
                  pl.BlockSpec((1, nh2), lambda b, c: (0, 0)),
                  pl.BlockSpec((1, nh2), lambda b, c: (0, 0))],
        out_specs=[pl.BlockSpec((1, SSD_CHUNK, nh2), lambda b, c: (b, c, 0)),
                   pl.BlockSpec((1, SSD_CHUNK, nh2), lambda b, c: (b, c, 0))],
        out_shape=[out, out],
        compiler_params=_params("parallel", "parallel"),
        name="ssd_dt",
    )(proj, dt_bias.reshape(1, nh2), a_log.reshape(1, nh2))


_GROUP_W = 512
_HEADS_PER_GROUP = 8
_PAIR_W = 2 * SSD_HEAD_DIM
_SSD_GROUPS_PER_STEP = 2


def _ssd_direction(d, g, gi, x_ref, b_ref, c_ref, dt_ref, cs_ref, cst_ref, dsk_ref, st_ref):
    n = SSD_CHUNK
    x = x_ref[0, :, gi * _GROUP_W:(gi + 1) * _GROUP_W]
    bm = b_ref[0, :, gi * SSD_STATE:(gi + 1) * SSD_STATE]
    cm = c_ref[0, :, gi * SSD_STATE:(gi + 1) * SSD_STATE].astype(BF16)
    dsk = dsk_ref[0, :, gi * _GROUP_W:(gi + 1) * _GROUP_W]
    n_heads = dt_ref.shape[2] // 2
    hbase = d * n_heads + g * _HEADS_PER_GROUP
    shift = (LANES - hbase) % LANES
    dtg = pltpu.roll(dt_ref[0], shift, 1)
    csg = pltpu.roll(cs_ref[0], shift, 1)
    bt = bm.T.astype(BF16)
    scores = _dot(cm, bt)
    ii = lax.broadcasted_iota(jnp.int32, (n, n), 0)
    jj = lax.broadcasted_iota(jnp.int32, (n, n), 1)
    mask = (ii >= jj) if d == 0 else (ii <= jj)
    low = jj < SSD_HEAD_DIM
    edge = n - 1 if d == 0 else 0
    st = st_ref[d, gi]
    y_inter = _dot(cm, st.astype(BF16))
    slabs = []
    for p in range(_HEADS_PER_GROUP // 2):
        h0, h1 = 2 * p, 2 * p + 1
        sl = slice(p * _PAIR_W, (p + 1) * _PAIR_W)
        c0 = jnp.broadcast_to(csg[:, h0:h0 + 1], (n, n))
        c1 = jnp.broadcast_to(csg[:, h1:h1 + 1], (n, n))
        r0 = cst_ref[0, pl.ds(hbase + h0, 1), :]
        r1 = cst_ref[0, pl.ds(hbase + h1, 1), :]
        l0 = jnp.exp(jnp.where(mask, c0 - r0, -jnp.inf))
        l1 = jnp.exp(jnp.where(mask, c1 - r1, -jnp.inf))
        m = jnp.concatenate([scores * l0, scores * l1], axis=1).astype(BF16)
        dts = jnp.where(low, jnp.broadcast_to(dtg[:, h0:h0 + 1], (n, n)), jnp.broadcast_to(dtg[:, h1:h1 + 1], (n, n)))
        css = jnp.where(low, c0, c1)
        tot = css[edge:edge + 1, :]
        xs = x[:, sl]
        xdt = xs * dts
        rhs = jnp.concatenate([jnp.where(low, xdt, 0.0), jnp.where(low, 0.0, xdt)], axis=0).astype(BF16)
        y = _dot(m, rhs) + y_inter[:, sl] * jnp.exp(css)
        if d == 0:
            y = y + xs * dsk[:, sl]
        slabs.append(y)
        wgt = (xdt * jnp.exp(tot - css)).astype(BF16)
        st_ref[d, gi, :, sl] = st[:, sl] * jnp.exp(tot) + _dot(bt, wgt)
    return jnp.concatenate(slabs, axis=1)


def _ssd_kernel(xf_ref, bf_ref, cf_ref, dtf_ref, csf_ref, cstf_ref,
                xb_ref, bb_ref, cb_ref, dtb_ref, csb_ref, cstb_ref,
                dsk_ref, h0_ref, y_ref, ht_ref, st_ref):
    gs = _SSD_GROUPS_PER_STEP
    g0 = pl.program_id(1) * gs
    c = pl.program_id(2)
    nc = pl.num_programs(2)

    @pl.when(c == 0)
    def _():
        st_ref[...] = h0_ref[0]

    yf = jnp.concatenate([_ssd_direction(0, g0 + gi, gi, xf_ref, bf_ref, cf_ref, dtf_ref, csf_ref, cstf_ref,
                                         dsk_ref, st_ref) for gi in range(gs)], axis=1)
    yb = jnp.concatenate([_ssd_direction(1, g0 + gi, gi, xb_ref, bb_ref, cb_ref, dtb_ref, csb_ref, cstb_ref,
                                         dsk_ref, st_ref) for gi in range(gs)], axis=1)
    rows_f = pl.ds(pl.multiple_of(c * SSD_CHUNK, SSD_CHUNK), SSD_CHUNK)
    rows_b = pl.ds(pl.multiple_of((nc - 1 - c) * SSD_CHUNK, SSD_CHUNK), SSD_CHUNK)

    @pl.when(c < nc // 2)
    def _():
        y_ref[0, rows_f, :] = yf
        y_ref[0, rows_b, :] = yb

    @pl.when(c >= nc // 2)
    def _():
        y_ref[0, rows_f, :] += yf
        y_ref[0, rows_b, :] += yb

    @pl.when(c == nc - 1)
    def _():
        ht_ref[0] = st_ref[...]


def _ssd_scan(xbc, dt, cs, cst, d_skip_row, h0):
    nb, l, _ = xbc.shape
    nc = l // SSD_CHUNK
    assert nc % 2 == 0
    n = SSD_CHUNK
    d_inner = SSD_GROUPS * _GROUP_W
    b_blk0 = d_inner // SSD_STATE
    c_blk0 = b_blk0 + SSD_GROUPS
    nh2 = dt.shape[2]
    gs = _SSD_GROUPS_PER_STEP
    assert SSD_GROUPS % gs == 0 and b_blk0 % gs == 0 and c_blk0 % gs == 0
    d_skip_row = d_skip_row.reshape(SSD_GROUPS // gs, 1, gs * _GROUP_W)

    def chunk_specs(which):
        return [pl.BlockSpec((1, n, gs * _GROUP_W), lambda b, g, c: (b, which(c), g)),
                pl.BlockSpec((1, n, gs * SSD_STATE), lambda b, g, c: (b, which(c), b_blk0 // gs + g)),
                pl.BlockSpec((1, n, gs * SSD_STATE), lambda b, g, c: (b, which(c), c_blk0 // gs + g)),
                pl.BlockSpec((1, n, nh2), lambda b, g, c: (b, which(c), 0)),
                pl.BlockSpec((1, n, nh2), lambda b, g, c: (b, which(c), 0)),
                pl.BlockSpec((1, nh2, n), lambda b, g, c: (b, 0, which(c)))]

    state_shape = jax.ShapeDtypeStruct(h0.shape, F32)
    return pl.pallas_call(
        _ssd_kernel,
        grid=(nb, SSD_GROUPS // gs, nc),
        in_specs=(chunk_specs(lambda c: c) + chunk_specs(lambda c: nc - 1 - c)
                  + [pl.BlockSpec((1, 1, gs * _GROUP_W), lambda b, g, c: (g, 0, 0)),
                     pl.BlockSpec((1, 2, gs, SSD_STATE, _GROUP_W), lambda b, g, c: (b, 0, g, 0, 0))]),
        out_specs=[pl.BlockSpec((1, l, gs * _GROUP_W), lambda b, g, c: (b, 0, g)),
                   pl.BlockSpec((1, 2, gs, SSD_STATE, _GROUP_W), lambda b, g, c: (b, 0, g, 0, 0))],
        out_shape=[jax.ShapeDtypeStruct((nb, l, d_inner), F32), state_shape],
        scratch_shapes=[pltpu.VMEM((2, gs, SSD_STATE, _GROUP_W), F32)],
        compiler_params=_params("parallel", "parallel", "arbitrary"),
        name="ssd_scan",
    )(xbc, xbc, xbc, dt, cs, cst, xbc, xbc, xbc, dt, cs, cst, d_skip_row, h0)


_LRU_CT = 512
_LRU_ROWS = 128


def _tile_scan(a, u, h_in, reverse):
    rows = lax.broadcasted_iota(jnp.int32, a.shape, 0)
    for s in (1, 2, 4):
        if reverse:
            keep = rows < SUBLANES - s
            amount = SUBLANES - s
        else:
            keep = rows >= s
            amount = s
        a_s = jnp.where(keep, pltpu.roll(a, amount, 0), 1.0)
        u_s = jnp.where(keep, pltpu.roll(u, amount, 0), 0.0)
        u = a * u_s + u
        a = a * a_s
    h = a * h_in + u
    return h, (h[0:1] if reverse else h[SUBLANES - 1:SUBLANES])


def _lru_kernel(rec_ref, w_ref, ba_ref, bi_ref, ap_ref, h0_ref, o_ref, ht_ref, a_scr, u_scr):
    l = rec_ref.shape[1]
    nch = l // _LRU_ROWS
    n_tiles = _LRU_ROWS // SUBLANES
    n_blk = _LRU_CT // LRU_BLOCK_W

    def gates(d, start):
        rec = rec_ref[0, pl.ds(start, _LRU_ROWS), :]
        sp = _softplus(-ap_ref[d])
        for k in range(n_blk):
            sl = slice(k * LRU_BLOCK_W, (k + 1) * LRU_BLOCK_W)
            rk = rec[:, sl]
            pre = _dot(rk.astype(BF16), w_ref[d, k])
            r = _sigmoid(pre[:, :LRU_BLOCK_W] + ba_ref[d][:, sl])
            ig = _sigmoid(pre[:, LRU_BLOCK_W:] + bi_ref[d][:, sl])
            log_a = -LRU_C * r * sp[:, sl]
            a = jnp.exp(log_a)
            a_scr[d, :, sl] = a
            u_scr[d, :, sl] = jnp.sqrt(1.0 - a * a) * (ig * rk)

    def chunk(c, carry, accumulate):
        hf, hb = carry
        start_f = pl.multiple_of(c * _LRU_ROWS, _LRU_ROWS)
        start_b = pl.multiple_of((nch - 1 - c) * _LRU_ROWS, _LRU_ROWS)
        gates(0, start_f)
        gates(1, start_b)
        for t in range(n_tiles):
            tf = t * SUBLANES
            tb = (n_tiles - 1 - t) * SUBLANES
            hs_f, hf = _tile_scan(a_scr[0, tf:tf + SUBLANES, :], u_scr[0, tf:tf + SUBLANES, :], hf, False)
            hs_b, hb = _tile_scan(a_scr[1, tb:tb + SUBLANES, :], u_scr[1, tb:tb + SUBLANES, :], hb, True)
            rows_f = pl.ds(pl.multiple_of(start_f + tf, SUBLANES), SUBLANES)
            rows_b = pl.ds(pl.multiple_of(start_b + tb, SUBLANES), SUBLANES)
            if accumulate:
                o_ref[0, rows_f, :] += hs_f
                o_ref[0, rows_b, :] += hs_b
            else:
                o_ref[0, rows_f, :] = hs_f
                o_ref[0, rows_b, :] = hs_b
        return hf, hb

    carry = (h0_ref[0, 0], h0_ref[0, 1])
    carry = lax.fori_loop(0, nch // 2, lambda c, cr: chunk(c, cr, False), carry)
    hf, hb = lax.fori_loop(nch // 2, nch, lambda c, cr: chunk(c, cr, True), carry)
    ht_ref[0, 0] = hf
    ht_ref[0, 1] = hb


def _lru_scan(rec, w_gate, b_a, b_i, a_param, h0):
    nb, l, w = rec.shape
    assert (l // _LRU_ROWS) % 2 == 0
    n_blk = _LRU_CT // LRU_BLOCK_W
    vec = pl.BlockSpec((2, 1, _LRU_CT), lambda b, j: (0, 0, j))
    return pl.pallas_call(
        _lru_kernel,
        grid=(nb, w // _LRU_CT),
        in_specs=[pl.BlockSpec((1, l, _LRU_CT), lambda b, j: (b, 0, j)),
                  pl.BlockSpec((2, n_blk, LRU_BLOCK_W, 2 * LRU_BLOCK_W), lambda b, j: (0, j, 0, 0)),
                  vec, vec, vec,
                  pl.BlockSpec((1, 2, 1, _LRU_CT), lambda b, j: (b, 0, 0, j))],
        out_specs=[pl.BlockSpec((1, l, _LRU_CT), lambda b, j: (b, 0, j)),
                   pl.BlockSpec((1, 2, 1, _LRU_CT), lambda b, j: (b, 0, 0, j))],
        out_shape=[jax.ShapeDtypeStruct((nb, l, w), F32), jax.ShapeDtypeStruct((nb, 2, 1, w), F32)],
        scratch_shapes=[pltpu.VMEM((2, _LRU_ROWS, _LRU_CT), F32), pltpu.VMEM((2, _LRU_ROWS, _LRU_CT), F32)],
        compiler_params=_params("parallel", "parallel"),
        name="lru_scan",
    )(rec, w_gate, b_a, b_i, a_param, h0)


def _final_norm_kernel(x_ref, w_ref, o_ref):
    x = x_ref[0]
    o_ref[0] = x * lax.rsqrt(jnp.mean(x * x, axis=-1, keepdims=True) + EPS) * w_ref[...]


def _final_norm(x, w, *, tm=512):
    nb, l, d = x.shape
    tm = min(tm, l)
    return pl.pallas_call(
        _final_norm_kernel,
        grid=(nb, l // tm),
        in_specs=[pl.BlockSpec((1, tm, d), lambda b, i: (b, i, 0)), pl.BlockSpec((1, d), lambda b, i: (0, 0))],
        out_specs=pl.BlockSpec((1, tm, d), lambda b, i: (b, i, 0)),
        out_shape=jax.ShapeDtypeStruct((nb, l, d), F32),
        compiler_params=_params("parallel", "parallel"),
        name="final_norm",
    )(x, w.reshape(1, d))


def _rope_tables(length):
    pos = jnp.arange(length)
    row = (pos // GRID_W).astype(F32)
    col = (pos % GRID_W).astype(F32)
    n_freq = HEAD_DIM // 4
    inv = ROPE_THETA ** (-jnp.arange(n_freq, dtype=F32) / n_freq)
    ang = jnp.concatenate([row[:, None] * inv, col[:, None] * inv], axis=-1)
    cos, sin = jnp.cos(ang), jnp.sin(ang)
    return jnp.concatenate([cos, cos], axis=-1), jnp.concatenate([-sin, sin], axis=-1)


def _attn_mixer(x, seq_shape, mods, nw, w_qkv, head_norms, w_o, rope, k_ctx, v_ctx):
    shift, scale, gate = mods
    nb, l, d = x.shape
    b, s = seq_shape
    use_rope = rope is not None
    cos_t, sin_t = rope if use_rope else (jnp.zeros((l, HEAD_DIM), F32), jnp.zeros((l, HEAD_DIM), F32))
    qkv = _qkv_proj(x, nw, shift, scale, w_qkv, head_norms, cos_t, sin_t, rope=use_rope,
                    out_dtype=BF16 if use_rope else F32)
    qkv_seq = qkv.reshape(b, s, qkv.shape[-1])
    o = _attention(qkv_seq, k_ctx, v_ctx)
    x = _mixer_out("attn", [o.reshape(nb, l, d)], w_o, x, gate)
    return x, qkv_seq


def _ssd_mixer(x, seq_shape, mods, nw, w_in, conv_w, conv_b, dt_bias, a_log, d_skip_row, norm_w, w_out, h0):
    shift, scale, gate = mods
    nb, l, d = x.shape
    b, s = seq_shape
    d_inner = w_out.shape[0]
    conv_dim = conv_w.shape[1]
    proj = _norm_mod_matmul(x, nw, shift, scale, w_in, name="ssd_in")
    proj_seq = proj.reshape(b, s, proj.shape[-1])
    xbc = _dwconv(proj_seq, d_inner, conv_dim, conv_w, conv_b, silu=True)
    dt, cs = _ssd_dt(proj_seq, d_inner + conv_dim, dt_bias, a_log)
    y, h_t = _ssd_scan(xbc, dt, cs, jnp.swapaxes(cs, 1, 2), d_skip_row, h0)
    hn = _gated_norm(y.reshape(nb, l, d_inner), proj, norm_w)
    x = _mixer_out("ssd", [hn], w_out, x, gate)
    return x, h_t


def _lru_mixer(x, seq_shape, mods, nw, w_in, conv_w, conv_b, w_gate, b_a, b_i, a_param, w_out, h0):
    shift, scale, gate = mods
    nb, l, d = x.shape
    b, s = seq_shape
    width = w_out.shape[0]
    proj = _norm_mod_matmul(x, nw, shift, scale, w_in, name="lru_in")
    rec = _dwconv(proj.reshape(b, s, 2 * width), width, width, conv_w, conv_b, silu=False)
    hs, h_t = _lru_scan(rec, w_gate, b_a, b_i, a_param, h0)
    x = _mixer_out("lru", [hs.reshape(nb, l, width), proj], w_out, x, gate)
    return x, h_t


def _states_to_kernel_layout(h):
    b, nh, p, n = h.shape
    return h.reshape(b, SSD_GROUPS, nh // SSD_GROUPS, p, n).transpose(0, 1, 4, 2, 3).reshape(b, SSD_GROUPS, n, -1)


def _states_from_kernel_layout(h, nh, p):
    b, two, g, n, _ = h.shape
    return h.reshape(b, two, g, n, nh // g, p).transpose(0, 1, 2, 4, 5, 3).reshape(b, two, nh, p, n)


def kernel(x_prompt, x_sample, cache_k, cache_v, state_ssm, state_lru, c, c_ctx, ada_w, ada_b, norm_mix, norm_ffn, attn_w_qkv, attn_q_norm, attn_k_norm, attn_w_o, ssd_w_in, ssd_conv_w, ssd_conv_b, ssd_dt_bias, ssd_a_log, ssd_d, ssd_norm, ssd_w_out, lru_w_in, lru_conv_w, lru_conv_b, lru_w_a, lru_b_a, lru_w_i, lru_b_i, lru_a_param, lru_w_out, ffn_w_in, ffn_w_out, final_norm):
    bp, sp, d = x_prompt.shape
    bs, ss, _ = x_sample.shape
    depth = ada_w.shape[0]
    n_mixers = 3

    pad_rows = -(bs + 1) % SUBLANES
    cond = jnp.concatenate([c, c_ctx[None, :], jnp.zeros((pad_rows, d), F32)], axis=0)
    mod = _adaln(cond, ada_w, ada_b).reshape(depth, bs + 1 + pad_rows, 6, 1, d)

    xp = x_prompt.reshape(1, bp * sp, d)
    xs = x_sample
    streams = {"p": (bp, sp), "s": (bs, ss)}
    rope = _rope_tables(ss)

    ssd_nh = ssd_d.shape[1]
    ssd_p = state_ssm.shape[4]
    new_k, new_v, new_ssm, new_lru = [], [], [], []
    for i in range(depth):
        kind, j = i % n_mixers, i // n_mixers
        mp = [mod[i, bs:bs + 1, t] for t in range(6)]
        ms = [mod[i, :bs, t] for t in range(6)]
        if kind == 0:
            w_qkv = attn_w_qkv[j].astype(BF16)
            w_o = attn_w_o[j].astype(BF16)
            head_norms = jnp.stack([attn_q_norm[j], attn_k_norm[j]]).reshape(2, 1, HEAD_DIM)
            kv_w = N_KV_HEADS * HEAD_DIM
            xp, qkv_p = _attn_mixer(xp, streams["p"], mp[:3], norm_mix[i], w_qkv, head_norms, w_o, None, None, None)
            xs, _ = _attn_mixer(xs, streams["s"], ms[:3], norm_mix[i], w_qkv, head_norms, w_o, rope,
                                cache_k[:, j], cache_v[:, j])
            q_w = qkv_p.shape[-1] - 2 * kv_w
            new_k.append(qkv_p[:, :, q_w:q_w + kv_w].reshape(bp, sp, N_KV_HEADS, HEAD_DIM))
            new_v.append(qkv_p[:, :, q_w + kv_w:].reshape(bp, sp, N_KV_HEADS, HEAD_DIM))
        elif kind == 1:
            w_in = ssd_w_in[j].astype(BF16)
            w_in = jnp.pad(w_in, ((0, 0), (0, -w_in.shape[1] % 512)))
            w_out = ssd_w_out[j].astype(BF16)
            d_skip_row = jnp.repeat(ssd_d[j], ssd_p).reshape(SSD_GROUPS, 1, _GROUP_W)
            args = (norm_mix[i], w_in, ssd_conv_w[j], ssd_conv_b[j], ssd_dt_bias[j], ssd_a_log[j], d_skip_row,
                    ssd_norm[j], w_out)
            zero = jnp.zeros((bp, 2, SSD_GROUPS, SSD_STATE, _GROUP_W), F32)
            h0 = jnp.stack([_states_to_kernel_layout(state_ssm[:, j, 0]),
                            _states_to_kernel_layout(state_ssm[:, j, 1])], axis=1)
            xp, h_t = _ssd_mixer(xp, streams["p"], mp[:3], *args, zero)
            xs, _ = _ssd_mixer(xs, streams["s"], ms[:3], *args, h0)
            new_ssm.append(_states_from_kernel_layout(h_t, ssd_nh, ssd_p))
        else:
            w_in = lru_w_in[j].astype(BF16)
            w_out = lru_w_out[j].astype(BF16)
            width = w_out.shape[0]
            w_gate = jnp.concatenate([lru_w_a[j], lru_w_i[j]], axis=-1).astype(BF16)
            args = (norm_mix[i], w_in, lru_conv_w[j], lru_conv_b[j], w_gate, lru_b_a[j].reshape(2, 1, width),
                    lru_b_i[j].reshape(2, 1, width), lru_a_param[j].reshape(2, 1, width), w_out)
            xp, h_t = _lru_mixer(xp, streams["p"], mp[:3], *args, jnp.zeros((bp, 2, 1, width), F32))
            xs, _ = _lru_mixer(xs, streams["s"], ms[:3], *args, state_lru[:, j].reshape(bs, 2, 1, width))
            new_lru.append(h_t.reshape(bp, 2, width))
        w1 = ffn_w_in[i].astype(BF16)
        w2 = ffn_w_out[i].astype(BF16)
        xp = _ffn(xp, norm_ffn[i], mp[3], mp[4], mp[5], w1, w2)
        xs = _ffn(xs, norm_ffn[i], ms[3], ms[4], ms[5], w1, w2)
    y_prompt = _final_norm(xp, final_norm).reshape(bp, sp, d)
    y_sample = _final_norm(xs, final_norm)
    return (y_prompt, y_sample, jnp.stack(new_k, axis=1), jnp.stack(new_v, axis=1),
            jnp.stack(new_ssm, axis=1), jnp.stack(new_lru, axis=1))
```

```python
import functools
import math

import jax
import jax.numpy as jnp
from jax import lax
from jax.experimental import pallas as pl
from jax.experimental.pallas import tpu as pltpu

F32 = jnp.float32
BF16 = jnp.bfloat16
EPS = 1e-6

GRID_W = 64
HEAD_DIM = 128
N_KV_HEADS = 4
Q_PER_KV = 4
ROPE_THETA = 10000.0
SSD_HEAD_DIM = 64
SSD_GROUPS = 8
SSD_STATE = 128
SSD_CHUNK = 128
LRU_BLOCK_W = 128
LRU_C = 8.0

LANES = 128
SUBLANES = 8
VMEM_LIMIT_BYTES = 56 * 1024 * 1024
VMEM_LIMIT_FFN_BYTES = 58 * 1024 * 1024


def _params(*sem, vmem=VMEM_LIMIT_BYTES):
    return pltpu.CompilerParams(dimension_semantics=sem, vmem_limit_bytes=vmem)


def _sigmoid(x):
    return 1.0 / (1.0 + jnp.exp(-x))


def _silu(x):
    return x * _sigmoid(x)


def _softplus(x):
    return jnp.maximum(x, 0.0) + jnp.log(1.0 + jnp.exp(-jnp.abs(x)))


def _dot(a, b):
    return jnp.dot(a, b, preferred_element_type=F32)


def _split3(v):
    v1 = v.astype(BF16)
    r1 = v - v1.astype(F32)
    v2 = r1.astype(BF16)
    v3 = (r1 - v2.astype(F32)).astype(BF16)
    return v1, v2, v3


def _adaln_kernel(c_ref, w_ref, b_ref, o_ref):
    s = _silu(c_ref[...]).astype(BF16)
    o_ref[0] = _dot(s, w_ref[0].astype(BF16)) + b_ref[0]


def _adaln(cond, ada_w, ada_b):
    depth, d, n = ada_w.shape
    rows = cond.shape[0]
    tn = 1024
    return pl.pallas_call(
        _adaln_kernel,
        grid=(depth, n // tn),
        in_specs=[pl.BlockSpec((rows, d), lambda l, j: (0, 0)),
                  pl.BlockSpec((1, d, tn), lambda l, j: (l, 0, j)),
                  pl.BlockSpec((1, 1, tn), lambda l, j: (l, 0, j))],
        out_specs=pl.BlockSpec((1, rows, tn), lambda l, j: (l, 0, j)),
        out_shape=jax.ShapeDtypeStruct((depth, rows, n), F32),
        compiler_params=_params("parallel", "arbitrary"),
        name="adaln",
    )(cond, ada_w, ada_b.reshape(depth, 1, n))


def _norm_modulate(x, nw, shift, scale):
    y = x * lax.rsqrt(jnp.mean(x * x, axis=-1, keepdims=True) + EPS)
    return ((y * nw) * (1.0 + scale) + shift).astype(BF16)


_ROW_CHUNK = 256


def _row_chunks(tm):
    rc = min(_ROW_CHUNK, tm)
    return [slice(r, r + rc) for r in range(0, tm, rc)]


def _nmm_kernel(x_ref, nw_ref, sh_ref, sc_ref, w_ref, o_ref, h_ref):
    j = pl.program_id(2)

    @pl.when(j == 0)
    def _():
        for rows in _row_chunks(h_ref.shape[0]):
            h = _norm_modulate(x_ref[0, rows, :], nw_ref[...], sh_ref[0], sc_ref[0])
            h_ref[rows, :] = h
            o_ref[0, rows, :] = _dot(h, w_ref[...]).astype(o_ref.dtype)

    @pl.when(j > 0)
    def _():
        o_ref[0] = _dot(h_ref[...], w_ref[...]).astype(o_ref.dtype)


def _norm_mod_matmul(x, nw, shift, scale, w, *, tm=1024, tn=512, out_dtype=F32, name="nmm"):
    nb, l, d = x.shape
    n = w.shape[1]
    tm = min(tm, l)
    return pl.pallas_call(
        _nmm_kernel,
        grid=(nb, l // tm, n // tn),
        in_specs=[pl.BlockSpec((1, tm, d), lambda b, i, j: (b, i, 0)),
                  pl.BlockSpec((1, d), lambda b, i, j: (0, 0)),
                  pl.BlockSpec((1, 1, d), lambda b, i, j: (b, 0, 0)),
                  pl.BlockSpec((1, 1, d), lambda b, i, j: (b, 0, 0)),
                  pl.BlockSpec((d, tn), lambda b, i, j: (0, j))],
        out_specs=pl.BlockSpec((1, tm, tn), lambda b, i, j: (b, i, j)),
        out_shape=jax.ShapeDtypeStruct((nb, l, n), out_dtype),
        scratch_shapes=[pltpu.VMEM((tm, d), BF16)],
        compiler_params=_params("parallel", "parallel", "arbitrary"),
        name=name,
    )(x, nw.reshape(1, d), shift, scale, w)


_QKV_TILE_HEADS = 2 * N_KV_HEADS
_QKV_TN = _QKV_TILE_HEADS * HEAD_DIM
_KV_TILE = (N_KV_HEADS * Q_PER_KV) // _QKV_TILE_HEADS


def _qkv_kernel(x_ref, nw_ref, sh_ref, sc_ref, w_ref, hn_ref, cos_ref, sin_ref, o_ref, h_ref, *, rope):
    j = pl.program_id(2)
    is_kv_tile = j == _KV_TILE
    head_norm = jnp.where(is_kv_tile, hn_ref[1], hn_ref[0])

    def emit(rows, h):
        acc = _dot(h, w_ref[...])
        for hh in range(_QKV_TILE_HEADS):
            cols = slice(hh * HEAD_DIM, (hh + 1) * HEAD_DIM)
            xh = acc[:, cols]
            xn = xh * lax.rsqrt(jnp.mean(xh * xh, axis=-1, keepdims=True) + EPS) * head_norm
            if rope:
                xn = xn * cos_ref[rows, :] + pltpu.roll(xn, HEAD_DIM // 2, 1) * sin_ref[rows, :]
            if hh >= N_KV_HEADS:
                xn = jnp.where(is_kv_tile, xh, xn)
            o_ref[0, rows, cols] = xn.astype(o_ref.dtype)

    @pl.when(j == 0)
    def _():
        for rows in _row_chunks(h_ref.shape[0]):
            h = _norm_modulate(x_ref[0, rows, :], nw_ref[...], sh_ref[0], sc_ref[0])
            h_ref[rows, :] = h
            emit(rows, h)

    @pl.when(j > 0)
    def _():
        for rows in _row_chunks(h_ref.shape[0]):
            emit(rows, h_ref[rows, :])


def _qkv_proj(x, nw, shift, scale, w, head_norms, cos_t, sin_t, *, rope, out_dtype, tm=1024):
    nb, l, d = x.shape
    n = w.shape[1]
    tm = min(tm, l)
    tn = _QKV_TN
    return pl.pallas_call(
        functools.partial(_qkv_kernel, rope=rope),
        grid=(nb, l // tm, n // tn),
        in_specs=[pl.BlockSpec((1, tm, d), lambda b, i, j: (b, i, 0)),
                  pl.BlockSpec((1, d), lambda b, i, j: (0, 0)),
                  pl.BlockSpec((1, 1, d), lambda b, i, j: (b, 0, 0)),
                  pl.BlockSpec((1, 1, d), lambda b, i, j: (b, 0, 0)),
                  pl.BlockSpec((d, tn), lambda b, i, j: (0, j)),
                  pl.BlockSpec((2, 1, HEAD_DIM), lambda b, i, j: (0, 0, 0)),
                  pl.BlockSpec((tm, HEAD_DIM), lambda b, i, j: (i, 0)),
                  pl.BlockSpec((tm, HEAD_DIM), lambda b, i, j: (i, 0))],
        out_specs=pl.BlockSpec((1, tm, tn), lambda b, i, j: (b, i, j)),
        out_shape=jax.ShapeDtypeStruct((nb, l, n), out_dtype),
        scratch_shapes=[pltpu.VMEM((tm, d), BF16)],
        compiler_params=_params("parallel", "parallel", "arbitrary"),
        name="qkv_proj",
    )(x, nw.reshape(1, d), shift, scale, w, head_norms, cos_t, sin_t)


_ATTN_KEY_BLOCK = 256


def _attn_kernel(q_ref, k_ref, vt_ref, o_ref):
    tq = q_ref.shape[1]
    cols = Q_PER_KV * tq
    n_blocks, _, tkb = vt_ref.shape[2:]
    q = q_ref[0].astype(F32) * (HEAD_DIM ** -0.5 * math.log2(math.e))
    q4 = jnp.concatenate([q[:, g * HEAD_DIM:(g + 1) * HEAD_DIM] for g in range(Q_PER_KV)], axis=0)
    qt = q4.T.astype(BF16)

    def scores(kb):
        return _dot(k_ref[0, 0, pl.ds(pl.multiple_of(kb * tkb, tkb), tkb), :], qt)

    def update(kb, s, m, l, acc):
        m_new = jnp.maximum(m, jnp.max(s, axis=0, keepdims=True))
        alpha = jnp.exp2(m - m_new)
        p = jnp.exp2(s - m_new)
        l = alpha * l + jnp.sum(p, axis=0, keepdims=True)
        acc = alpha * acc + _dot(vt_ref[0, 0, kb], p.astype(BF16))
        return m_new, l, acc

    unroll = next(u for u in (9, 6, 3, 2, 1) if n_blocks % u == 0)

    def body(i, carry):
        ss = [scores(i * unroll + u) for u in range(unroll)]
        for u in range(unroll):
            carry = update(i * unroll + u, ss[u], *carry)
        return carry

    init = (jnp.full((1, cols), -jnp.inf, F32), jnp.zeros((1, cols), F32), jnp.zeros((HEAD_DIM, cols), F32))
    _, l, acc = lax.fori_loop(0, n_blocks // unroll, body, init)
    o = (acc / l).T
    for g in range(Q_PER_KV):
        o_ref[0, :, g * HEAD_DIM:(g + 1) * HEAD_DIM] = o[g * tq:(g + 1) * tq].astype(o_ref.dtype)


def _attention(qkv, k_ctx=None, v_ctx=None, *, tq=256):
    b, l, _ = qkv.shape
    tq = min(tq, l)
    q_w = N_KV_HEADS * Q_PER_KV * HEAD_DIM
    kv_w = N_KV_HEADS * HEAD_DIM
    k = qkv[:, :, q_w:q_w + kv_w].astype(BF16).reshape(b, l, N_KV_HEADS, HEAD_DIM)
    v = qkv[:, :, q_w + kv_w:].astype(BF16).reshape(b, l, N_KV_HEADS, HEAD_DIM)
    if k_ctx is not None:
        k = jnp.concatenate([k, k_ctx.astype(BF16)], axis=1)
        v = jnp.concatenate([v, v_ctx.astype(BF16)], axis=1)
    lk = k.shape[1]
    tkb = _ATTN_KEY_BLOCK
    assert lk % tkb == 0, (lk, tkb)
    n_blocks = lk // tkb
    k = k.transpose(0, 2, 1, 3)
    vt = v.reshape(b, n_blocks, tkb, N_KV_HEADS, HEAD_DIM).transpose(0, 3, 1, 4, 2)
    return pl.pallas_call(
        _attn_kernel,
        grid=(b, N_KV_HEADS, l // tq),
        in_specs=[pl.BlockSpec((1, tq, Q_PER_KV * HEAD_DIM), lambda bi, h, qi: (bi, qi, h)),
                  pl.BlockSpec((1, 1, lk, HEAD_DIM), lambda bi, h, qi: (bi, h, 0, 0)),
                  pl.BlockSpec((1, 1, n_blocks, HEAD_DIM, tkb), lambda bi, h, qi: (bi, h, 0, 0, 0))],
        out_specs=pl.BlockSpec((1, tq, Q_PER_KV * HEAD_DIM), lambda bi, h, qi: (bi, qi, h)),
        out_shape=jax.ShapeDtypeStruct((b, l, q_w), BF16),
        compiler_params=_params("parallel", "parallel", "arbitrary"),
        name="attention",
    )(qkv, k, vt)


def _plain_out_kernel(a_ref, w_ref, x_ref, g_ref, o_ref):
    o_ref[0] = x_ref[0] + g_ref[0] * _dot(a_ref[0], w_ref[...])


def _lru_out_kernel(y_ref, gb_ref, w_ref, x_ref, g_ref, o_ref, h_ref):
    j = pl.program_id(2)

    @pl.when(j == 0)
    def _():
        for rows in _row_chunks(h_ref.shape[0]):
            h = (y_ref[0, rows, :] * jax.nn.gelu(gb_ref[0, rows, :], approximate=True)).astype(BF16)
            h_ref[rows, :] = h
            o_ref[0, rows, :] = x_ref[0, rows, :] + g_ref[0] * _dot(h, w_ref[...])

    @pl.when(j > 0)
    def _():
        o_ref[0] = x_ref[0] + g_ref[0] * _dot(h_ref[...], w_ref[...])


def _mixer_out(kind, acts, w, x, gate, *, tm=1024, tn=1024):
    nb, l, d = x.shape
    k = w.shape[0]
    tm = min(tm, l)
    act_specs = [pl.BlockSpec((1, tm, k), lambda b, i, j: (b, i, 0)) for _ in acts]
    tail_specs = [pl.BlockSpec((k, tn), lambda b, i, j: (0, j)),
                  pl.BlockSpec((1, tm, tn), lambda b, i, j: (b, i, j)),
                  pl.BlockSpec((1, 1, tn), lambda b, i, j: (b, 0, j))]
    if kind == "lru":
        body, scratch = _lru_out_kernel, [pltpu.VMEM((tm, k), BF16)]
    else:
        body, scratch = _plain_out_kernel, []
    return pl.pallas_call(
        body,
        grid=(nb, l // tm, d // tn),
        in_specs=act_specs + tail_specs,
        out_specs=pl.BlockSpec((1, tm, tn), lambda b, i, j: (b, i, j)),
        out_shape=jax.ShapeDtypeStruct((nb, l, d), F32),
        scratch_shapes=scratch,
        compiler_params=_params("parallel", "parallel", "arbitrary"),
        name=kind + "_out",
    )(*acts, w, x, gate)


def _gated_norm_kernel(y_ref, z_ref, nw_ref, o_ref):
    t = y_ref[0] * _silu(z_ref[0])
    t = t * lax.rsqrt(jnp.mean(t * t, axis=-1, keepdims=True) + EPS)
    o_ref[0] = (t * nw_ref[...]).astype(o_ref.dtype)


def _gated_norm(y, z_src, nw, *, tm=512):
    nb, l, k = y.shape
    tm = min(tm, l)
    return pl.pallas_call(
        _gated_norm_kernel,
        grid=(nb, l // tm),
        in_specs=[pl.BlockSpec((1, tm, k), lambda b, i: (b, i, 0)),
                  pl.BlockSpec((1, tm, k), lambda b, i: (b, i, 0)),
                  pl.BlockSpec((1, k), lambda b, i: (0, 0))],
        out_specs=pl.BlockSpec((1, tm, k), lambda b, i: (b, i, 0)),
        out_shape=jax.ShapeDtypeStruct((nb, l, k), BF16),
        compiler_params=_params("parallel", "parallel"),
        name="gated_norm",
    )(y, z_src, nw.reshape(1, k))


def _ffn_kernel(x_ref, nw_ref, sh_ref, sc_ref, g_ref, wg_ref, wu_ref, wo_ref, o_ref, h_ref):
    j = pl.program_id(2)

    def partial_out(h):
        a = (_silu(_dot(h, wg_ref[...])) * _dot(h, wu_ref[...])).astype(BF16)
        return _dot(a, wo_ref[...])

    @pl.when(j == 0)
    def _():
        for rows in _row_chunks(h_ref.shape[0]):
            h = _norm_modulate(x_ref[0, rows, :], nw_ref[...], sh_ref[0], sc_ref[0])
            h_ref[rows, :] = h
            o_ref[0, rows, :] = partial_out(h)

    @pl.when(j > 0)
    def _():
        o_ref[0] += partial_out(h_ref[...])

    @pl.when(j == pl.num_programs(2) - 1)
    def _():
        o_ref[0] = x_ref[0] + g_ref[0] * o_ref[0]


def _ffn(x, nw, shift, scale, gate, w_in, w_out, *, tm=1024, tf=512):
    nb, l, d = x.shape
    f = w_out.shape[0]
    tm = min(tm, l)
    nf = f // tf
    return pl.pallas_call(
        _ffn_kernel,
        grid=(nb, l // tm, nf),
        in_specs=[pl.BlockSpec((1, tm, d), lambda b, i, j: (b, i, 0), pipeline_mode=pl.Buffered(1)),
                  pl.BlockSpec((1, d), lambda b, i, j: (0, 0)),
                  pl.BlockSpec((1, 1, d), lambda b, i, j: (b, 0, 0)),
                  pl.BlockSpec((1, 1, d), lambda b, i, j: (b, 0, 0)),
                  pl.BlockSpec((1, 1, d), lambda b, i, j: (b, 0, 0)),
                  pl.BlockSpec((d, tf), lambda b, i, j: (0, j)),
                  pl.BlockSpec((d, tf), lambda b, i, j: (0, nf + j)),
                  pl.BlockSpec((tf, d), lambda b, i, j: (j, 0))],
        out_specs=pl.BlockSpec((1, tm, d), lambda b, i, j: (b, i, 0)),
        out_shape=jax.ShapeDtypeStruct((nb, l, d), F32),
        scratch_shapes=[pltpu.VMEM((tm, d), BF16)],
        compiler_params=_params("parallel", "parallel", "arbitrary", vmem=VMEM_LIMIT_FFN_BYTES),
        name="ffn",
    )(x, nw.reshape(1, d), shift, scale, gate, w_in, w_in, w_out)


def _conv_kernel(prev_ref, cur_ref, nxt_ref, w_ref, b_ref, o_ref, buf_ref, *, silu):
    i = pl.program_id(1)
    tl = cur_ref.shape[1]
    h = SUBLANES
    cur = cur_ref[0]
    buf_ref[0:h] = jnp.where(i > 0, prev_ref[0], 0.0)
    buf_ref[h:h + tl] = cur
    buf_ref[h + tl:h + tl + h] = jnp.where(i < pl.num_programs(1) - 1, nxt_ref[0], 0.0)
    w = w_ref[...]
    y = (buf_ref[h - 1:h - 1 + tl] * w[0:1] + cur * w[1:2] + buf_ref[h + 1:h + 1 + tl] * w[2:3]
         + buf_ref[h + 2:h + 2 + tl] * w[3:4] + b_ref[...])
    o_ref[0] = _silu(y) if silu else y


def _dwconv(src, col0, width, w, bias, *, silu, tl=1024, ct=1024):
    nb, l, _ = src.shape
    tl = min(tl, l)
    c0 = col0 // ct
    rb = tl // SUBLANES
    last = l // SUBLANES - 1
    return pl.pallas_call(
        functools.partial(_conv_kernel, silu=silu),
        grid=(nb, l // tl, width // ct),
        in_specs=[pl.BlockSpec((1, SUBLANES, ct), lambda b, i, j: (b, jnp.maximum(i * rb - 1, 0), c0 + j)),
                  pl.BlockSpec((1, tl, ct), lambda b, i, j: (b, i, c0 + j)),
                  pl.BlockSpec((1, SUBLANES, ct), lambda b, i, j: (b, jnp.minimum((i + 1) * rb, last), c0 + j)),
                  pl.BlockSpec((4, ct), lambda b, i, j: (0, j)),
                  pl.BlockSpec((1, ct), lambda b, i, j: (0, j))],
        out_specs=pl.BlockSpec((1, tl, ct), lambda b, i, j: (b, i, j)),
        out_shape=jax.ShapeDtypeStruct((nb, l, width), F32),
        scratch_shapes=[pltpu.VMEM((tl + 2 * SUBLANES, ct), F32)],
        compiler_params=_params("parallel", "parallel", "parallel"),
        name="dwconv",
    )(src, src, src, w, bias.reshape(1, width))


def _ssd_dt_kernel(raw_ref, bias_ref, alog_ref, dt_ref, cs_ref):
    n = SSD_CHUNK
    dt = _softplus(raw_ref[0] + bias_ref[...])
    da = dt * (-jnp.exp(alog_ref[...]))
    ii = lax.broadcasted_iota(jnp.int32, (n, n), 0)
    jj = lax.broadcasted_iota(jnp.int32, (n, n), 1)
    lower = (ii >= jj).astype(BF16)
    upper = (ii <= jj).astype(BF16)
    parts = _split3(da)
    pre = sum(_dot(lower, p) for p in parts)
    suf = sum(_dot(upper, p) for p in parts)
    lane = lax.broadcasted_iota(jnp.int32, pre.shape, 1)
    dt_ref[0] = dt
    cs_ref[0] = jnp.where(lane < pre.shape[1] // 2, pre, suf)


def _ssd_dt(proj, col0, dt_bias, a_log):
    nb, l, _ = proj.shape
    nh2 = dt_bias.size
    out = jax.ShapeDtypeStruct((nb, l, nh2), F32)
    return pl.pallas_call(
        _ssd_dt_kernel,
        grid=(nb, l // SSD_CHUNK),
        in_specs=[pl.BlockSpec((1, SSD_CHUNK, nh2), lambda b, c: (b, c, col0 // nh2)),
                  pl.BlockSpec((1, nh2), lambda b, c: (0, 0)),
                  pl.BlockSpec((1, nh2), lambda b, c: (0, 0))],
        out_specs=[pl.BlockSpec((1, SSD_CHUNK, nh2), lambda b, c: (b, c, 0)),
                   pl.BlockSpec((1, SSD_CHUNK, nh2), lambda b, c: (b, c, 0))],
        out_shape=[out, out],
        compiler_params=_params("parallel", "parallel"),
        name="ssd_dt",
    )(proj, dt_bias.reshape(1, nh2), a_log.reshape(1, nh2))


_GROUP_W = 512
_HEADS_PER_GROUP = 8
_PAIR_W = 2 * SSD_HEAD_DIM
_SSD_GROUPS_PER_STEP = 2
_SSD_IN_TN = 1536


def _ssd_direction(d, g, gi, x_ref, b_ref, c_ref, dt_ref, cs_ref, cst_ref, dsk_ref, st_ref):
    n = SSD_CHUNK
    x = x_ref[0, :, gi * _GROUP_W:(gi + 1) * _GROUP_W]
    bm = b_ref[0, :, gi * SSD_STATE:(gi + 1) * SSD_STATE]
    cm = c_ref[0, :, gi * SSD_STATE:(gi + 1) * SSD_STATE].astype(BF16)
    dsk = dsk_ref[0, :, gi * _GROUP_W:(gi + 1) * _GROUP_W]
    n_heads = cs_ref.shape[2] // 2
    hbase = d * n_heads + g * _HEADS_PER_GROUP
    csg = pltpu.roll(cs_ref[0], (LANES - hbase) % LANES, 1)
    bt = bm.T
    scores = _dot(cm, bt.astype(BF16))
    ii = lax.broadcasted_iota(jnp.int32, (n, n), 0)
    jj = lax.broadcasted_iota(jnp.int32, (n, n), 1)
    mask = (ii >= jj) if d == 0 else (ii <= jj)
    low = jj < SSD_HEAD_DIM
    edge = n - 1 if d == 0 else 0
    st = st_ref[d, gi]
    y_inter = _dot(cm, st.astype(BF16))
    slabs = []
    for p in range(_HEADS_PER_GROUP // 2):
        h0, h1 = 2 * p, 2 * p + 1
        sl = slice(p * _PAIR_W, (p + 1) * _PAIR_W)
        c0 = jnp.broadcast_to(csg[:, h0:h0 + 1], (n, n))
        c1 = jnp.broadcast_to(csg[:, h1:h1 + 1], (n, n))
        r0 = cst_ref[0, pl.ds(hbase + h0, 1), :]
        r1 = cst_ref[0, pl.ds(hbase + h1, 1), :]
        d0 = dt_ref[0, pl.ds(hbase + h0, 1), :]
        d1 = dt_ref[0, pl.ds(hbase + h1, 1), :]
        l0 = jnp.exp(jnp.where(mask, c0 - r0, -jnp.inf))
        l1 = jnp.exp(jnp.where(mask, c1 - r1, -jnp.inf))
        m = jnp.concatenate([scores * (l0 * d0), scores * (l1 * d1)], axis=1).astype(BF16)
        xs = x[:, sl]
        xb = jnp.concatenate([jnp.where(low, xs, 0.0), jnp.where(low, 0.0, xs)], axis=0).astype(BF16)
        css = jnp.where(low, c0, c1)
        y = _dot(m, xb) + y_inter[:, sl] * jnp.exp(css)
        if d == 0:
            y = y + xs * dsk[:, sl]
        slabs.append(y)
        w0 = d0 * jnp.exp(r0[:, edge:edge + 1] - r0)
        w1 = d1 * jnp.exp(r1[:, edge:edge + 1] - r1)
        bw = jnp.concatenate([bt * w0, bt * w1], axis=1).astype(BF16)
        st_ref[d, gi, :, sl] = st[:, sl] * jnp.exp(css[edge:edge + 1, :]) + _dot(bw, xb)
    return jnp.concatenate(slabs, axis=1)


def _ssd_kernel(xf_ref, bf_ref, cf_ref, dtf_ref, csf_ref, cstf_ref,
                xb_ref, bb_ref, cb_ref, dtb_ref, csb_ref, cstb_ref,
                dsk_ref, h0_ref, y_ref, ht_ref, st_ref):
    gs = _SSD_GROUPS_PER_STEP
    g0 = pl.program_id(1) * gs
    c = pl.program_id(2)
    nc = pl.num_programs(2)

    @pl.when(c == 0)
    def _():
        st_ref[...] = h0_ref[0]

    yf = jnp.concatenate([_ssd_direction(0, g0 + gi, gi, xf_ref, bf_ref, cf_ref, dtf_ref, csf_ref, cstf_ref,
                                         dsk_ref, st_ref) for gi in range(gs)], axis=1)
    yb = jnp.concatenate([_ssd_direction(1, g0 + gi, gi, xb_ref, bb_ref, cb_ref, dtb_ref, csb_ref, cstb_ref,
                                         dsk_ref, st_ref) for gi in range(gs)], axis=1)
    rows_f = pl.ds(pl.multiple_of(c * SSD_CHUNK, SSD_CHUNK), SSD_CHUNK)
    rows_b = pl.ds(pl.multiple_of((nc - 1 - c) * SSD_CHUNK, SSD_CHUNK), SSD_CHUNK)

    @pl.when(c < nc // 2)
    def _():
        y_ref[0, rows_f, :] = yf
        y_ref[0, rows_b, :] = yb

    @pl.when(c >= nc // 2)
    def _():
        y_ref[0, rows_f, :] += yf
        y_ref[0, rows_b, :] += yb

    @pl.when(c == nc - 1)
    def _():
        ht_ref[0] = st_ref[...]


def _ssd_scan(xbc, dtt, cs, cst, d_skip_row, h0):
    nb, l, _ = xbc.shape
    nc = l // SSD_CHUNK
    assert nc % 2 == 0
    n = SSD_CHUNK
    d_inner = SSD_GROUPS * _GROUP_W
    b_blk0 = d_inner // SSD_STATE
    c_blk0 = b_blk0 + SSD_GROUPS
    nh2 = cs.shape[2]
    gs = _SSD_GROUPS_PER_STEP
    assert SSD_GROUPS % gs == 0 and b_blk0 % gs == 0 and c_blk0 % gs == 0
    d_skip_row = d_skip_row.reshape(SSD_GROUPS // gs, 1, gs * _GROUP_W)

    def chunk_specs(which):
        return [pl.BlockSpec((1, n, gs * _GROUP_W), lambda b, g, c: (b, which(c), g)),
                pl.BlockSpec((1, n, gs * SSD_STATE), lambda b, g, c: (b, which(c), b_blk0 // gs + g)),
                pl.BlockSpec((1, n, gs * SSD_STATE), lambda b, g, c: (b, which(c), c_blk0 // gs + g)),
                pl.BlockSpec((1, nh2, n), lambda b, g, c: (b, 0, which(c))),
                pl.BlockSpec((1, n, nh2), lambda b, g, c: (b, which(c), 0)),
                pl.BlockSpec((1, nh2, n), lambda b, g, c: (b, 0, which(c)))]

    state_shape = jax.ShapeDtypeStruct(h0.shape, F32)
    return pl.pallas_call(
        _ssd_kernel,
        grid=(nb, SSD_GROUPS // gs, nc),
        in_specs=(chunk_specs(lambda c: c) + chunk_specs(lambda c: nc - 1 - c)
                  + [pl.BlockSpec((1, 1, gs * _GROUP_W), lambda b, g, c: (g, 0, 0)),
                     pl.BlockSpec((1, 2, gs, SSD_STATE, _GROUP_W), lambda b, g, c: (b, 0, g, 0, 0))]),
        out_specs=[pl.BlockSpec((1, l, gs * _GROUP_W), lambda b, g, c: (b, 0, g)),
                   pl.BlockSpec((1, 2, gs, SSD_STATE, _GROUP_W), lambda b, g, c: (b, 0, g, 0, 0))],
        out_shape=[jax.ShapeDtypeStruct((nb, l, d_inner), F32), state_shape],
        scratch_shapes=[pltpu.VMEM((2, gs, SSD_STATE, _GROUP_W), F32)],
        compiler_params=_params("parallel", "parallel", "arbitrary"),
        name="ssd_scan",
    )(xbc, xbc, xbc, dtt, cs, cst, xbc, xbc, xbc, dtt, cs, cst, d_skip_row, h0)


_LRU_CT = 512
_LRU_ROWS = 128


def _tile_scan(a, u, h_in, reverse):
    rows = lax.broadcasted_iota(jnp.int32, a.shape, 0)
    for s in (1, 2, 4):
        if reverse:
            keep = rows < SUBLANES - s
            amount = SUBLANES - s
        else:
            keep = rows >= s
            amount = s
        a_s = jnp.where(keep, pltpu.roll(a, amount, 0), 1.0)
        u_s = jnp.where(keep, pltpu.roll(u, amount, 0), 0.0)
        u = a * u_s + u
        a = a * a_s
    h = a * h_in + u
    return h, (h[0:1] if reverse else h[SUBLANES - 1:SUBLANES])


def _lru_kernel(rec_ref, w_ref, ba_ref, bi_ref, ap_ref, h0_ref, o_ref, ht_ref, a_scr, u_scr):
    l = rec_ref.shape[1]
    nch = l // _LRU_ROWS
    n_tiles = _LRU_ROWS // SUBLANES
    n_blk = _LRU_CT // LRU_BLOCK_W

    def gates(d, start):
        rec = rec_ref[0, pl.ds(start, _LRU_ROWS), :]
        sp = _softplus(-ap_ref[d])
        for k in range(n_blk):
            sl = slice(k * LRU_BLOCK_W, (k + 1) * LRU_BLOCK_W)
            rk = rec[:, sl]
            pre = _dot(rk.astype(BF16), w_ref[d, k])
            r = _sigmoid(pre[:, :LRU_BLOCK_W] + ba_ref[d][:, sl])
            ig = _sigmoid(pre[:, LRU_BLOCK_W:] + bi_ref[d][:, sl])
            log_a = -LRU_C * r * sp[:, sl]
            a = jnp.exp(log_a)
            a_scr[d, :, sl] = a
            u_scr[d, :, sl] = jnp.sqrt(1.0 - a * a) * (ig * rk)

    def chunk(c, carry, accumulate):
        hf, hb = carry
        start_f = pl.multiple_of(c * _LRU_ROWS, _LRU_ROWS)
        start_b = pl.multiple_of((nch - 1 - c) * _LRU_ROWS, _LRU_ROWS)
        gates(0, start_f)
        gates(1, start_b)
        for t in range(n_tiles):
            tf = t * SUBLANES
            tb = (n_tiles - 1 - t) * SUBLANES
            hs_f, hf = _tile_scan(a_scr[0, tf:tf + SUBLANES, :], u_scr[0, tf:tf + SUBLANES, :], hf, False)
            hs_b, hb = _tile_scan(a_scr[1, tb:tb + SUBLANES, :], u_scr[1, tb:tb + SUBLANES, :], hb, True)
            rows_f = pl.ds(pl.multiple_of(start_f + tf, SUBLANES), SUBLANES)
            rows_b = pl.ds(pl.multiple_of(start_b + tb, SUBLANES), SUBLANES)
            if accumulate:
                o_ref[0, rows_f, :] += hs_f
                o_ref[0, rows_b, :] += hs_b
            else:
                o_ref[0, rows_f, :] = hs_f
                o_ref[0, rows_b, :] = hs_b
        return hf, hb

    carry = (h0_ref[0, 0], h0_ref[0, 1])
    carry = lax.fori_loop(0, nch // 2, lambda c, cr: chunk(c, cr, False), carry)
    hf, hb = lax.fori_loop(nch // 2, nch, lambda c, cr: chunk(c, cr, True), carry)
    ht_ref[0, 0] = hf
    ht_ref[0, 1] = hb


def _lru_scan(rec, w_gate, b_a, b_i, a_param, h0):
    nb, l, w = rec.shape
    assert (l // _LRU_ROWS) % 2 == 0
    n_blk = _LRU_CT // LRU_BLOCK_W
    vec = pl.BlockSpec((2, 1, _LRU_CT), lambda b, j: (0, 0, j))
    return pl.pallas_call(
        _lru_kernel,
        grid=(nb, w // _LRU_CT),
        in_specs=[pl.BlockSpec((1, l, _LRU_CT), lambda b, j: (b, 0, j)),
                  pl.BlockSpec((2, n_blk, LRU_BLOCK_W, 2 * LRU_BLOCK_W), lambda b, j: (0, j, 0, 0)),
                  vec, vec, vec,
                  pl.BlockSpec((1, 2, 1, _LRU_CT), lambda b, j: (b, 0, 0, j))],
        out_specs=[pl.BlockSpec((1, l, _LRU_CT), lambda b, j: (b, 0, j)),
                   pl.BlockSpec((1, 2, 1, _LRU_CT), lambda b, j: (b, 0, 0, j))],
        out_shape=[jax.ShapeDtypeStruct((nb, l, w), F32), jax.ShapeDtypeStruct((nb, 2, 1, w), F32)],
        scratch_shapes=[pltpu.VMEM((2, _LRU_ROWS, _LRU_CT), F32), pltpu.VMEM((2, _LRU_ROWS, _LRU_CT), F32)],
        compiler_params=_params("parallel", "parallel"),
        name="lru_scan",
    )(rec, w_gate, b_a, b_i, a_param, h0)


def _final_norm_kernel(x_ref, w_ref, o_ref):
    x = x_ref[0]
    o_ref[0] = x * lax.rsqrt(jnp.mean(x * x, axis=-1, keepdims=True) + EPS) * w_ref[...]


def _final_norm(x, w, *, tm=512):
    nb, l, d = x.shape
    tm = min(tm, l)
    return pl.pallas_call(
        _final_norm_kernel,
        grid=(nb, l // tm),
        in_specs=[pl.BlockSpec((1, tm, d), lambda b, i: (b, i, 0)), pl.BlockSpec((1, d), lambda b, i: (0, 0))],
        out_specs=pl.BlockSpec((1, tm, d), lambda b, i: (b, i, 0)),
        out_shape=jax.ShapeDtypeStruct((nb, l, d), F32),
        compiler_params=_params("parallel", "parallel"),
        name="final_norm",
    )(x, w.reshape(1, d))


def _rope_tables(length):
    pos = jnp.arange(length)
    row = (pos // GRID_W).astype(F32)
    col = (pos % GRID_W).astype(F32)
    n_freq = HEAD_DIM // 4
    inv = ROPE_THETA ** (-jnp.arange(n_freq, dtype=F32) / n_freq)
    ang = jnp.concatenate([row[:, None] * inv, col[:, None] * inv], axis=-1)
    cos, sin = jnp.cos(ang), jnp.sin(ang)
    return jnp.concatenate([cos, cos], axis=-1), jnp.concatenate([-sin, sin], axis=-1)


def _attn_mixer(x, seq_shape, mods, nw, w_qkv, head_norms, w_o, rope, k_ctx, v_ctx):
    shift, scale, gate = mods
    nb, l, d = x.shape
    b, s = seq_shape
    use_rope = rope is not None
    cos_t, sin_t = rope if use_rope else (jnp.zeros((l, HEAD_DIM), F32), jnp.zeros((l, HEAD_DIM), F32))
    qkv = _qkv_proj(x, nw, shift, scale, w_qkv, head_norms, cos_t, sin_t, rope=use_rope,
                    out_dtype=BF16 if use_rope else F32)
    qkv_seq = qkv.reshape(b, s, qkv.shape[-1])
    o = _attention(qkv_seq, k_ctx, v_ctx)
    x = _mixer_out("attn", [o.reshape(nb, l, d)], w_o, x, gate)
    return x, qkv_seq


def _ssd_mixer(x, seq_shape, mods, nw, w_in, conv_w, conv_b, dt_bias, a_log, d_skip_row, norm_w, w_out, h0):
    shift, scale, gate = mods
    nb, l, d = x.shape
    b, s = seq_shape
    d_inner = w_out.shape[0]
    conv_dim = conv_w.shape[1]
    proj = _norm_mod_matmul(x, nw, shift, scale, w_in, tn=_SSD_IN_TN, name="ssd_in")
    proj_seq = proj.reshape(b, s, proj.shape[-1])
    xbc = _dwconv(proj_seq, d_inner, conv_dim, conv_w, conv_b, silu=True)
    dt, cs = _ssd_dt(proj_seq, d_inner + conv_dim, dt_bias, a_log)
    y, h_t = _ssd_scan(xbc, jnp.swapaxes(dt, 1, 2), cs, jnp.swapaxes(cs, 1, 2), d_skip_row, h0)
    hn = _gated_norm(y.reshape(nb, l, d_inner), proj, norm_w)
    x = _mixer_out("ssd", [hn], w_out, x, gate, tm=512)
    return x, h_t


def _lru_mixer(x, seq_shape, mods, nw, w_in, conv_w, conv_b, w_gate, b_a, b_i, a_param, w_out, h0):
    shift, scale, gate = mods
    nb, l, d = x.shape
    b, s = seq_shape
    width = w_out.shape[0]
    proj = _norm_mod_matmul(x, nw, shift, scale, w_in, tn=1024, name="lru_in")
    rec = _dwconv(proj.reshape(b, s, 2 * width), width, width, conv_w, conv_b, silu=False)
    hs, h_t = _lru_scan(rec, w_gate, b_a, b_i, a_param, h0)
    x = _mixer_out("lru", [hs.reshape(nb, l, width), proj], w_out, x, gate, tm=512, tn=d)
    return x, h_t


def _states_to_kernel_layout(h):
    b, nh, p, n = h.shape
    return h.reshape(b, SSD_GROUPS, nh // SSD_GROUPS, p, n).transpose(0, 1, 4, 2, 3).reshape(b, SSD_GROUPS, n, -1)


def _states_from_kernel_layout(h, nh, p):
    b, two, g, n, _ = h.shape
    return h.reshape(b, two, g, n, nh // g, p).transpose(0, 1, 2, 4, 5, 3).reshape(b, two, nh, p, n)


def kernel(x_prompt, x_sample, cache_k, cache_v, state_ssm, state_lru, c, c_ctx, ada_w, ada_b, norm_mix, norm_ffn, attn_w_qkv, attn_q_norm, attn_k_norm, attn_w_o, ssd_w_in, ssd_conv_w, ssd_conv_b, ssd_dt_bias, ssd_a_log, ssd_d, ssd_norm, ssd_w_out, lru_w_in, lru_conv_w, lru_conv_b, lru_w_a, lru_b_a, lru_w_i, lru_b_i, lru_a_param, lru_w_out, ffn_w_in, ffn_w_out, final_norm):
    bp, sp, d = x_prompt.shape
    bs, ss, _ = x_sample.shape
    depth = ada_w.shape[0]
    n_mixers = 3

    pad_rows = -(bs + 1) % SUBLANES
    cond = jnp.concatenate([c, c_ctx[None, :], jnp.zeros((pad_rows, d), F32)], axis=0)
    mod = _adaln(cond, ada_w, ada_b).reshape(depth, bs + 1 + pad_rows, 6, 1, d)

    xp = x_prompt.reshape(1, bp * sp, d)
    xs = x_sample
    streams = {"p": (bp, sp), "s": (bs, ss)}
    rope = _rope_tables(ss)

    ssd_nh = ssd_d.shape[1]
    ssd_p = state_ssm.shape[4]
    new_k, new_v, new_ssm, new_lru = [], [], [], []
    for i in range(depth):
        kind, j = i % n_mixers, i // n_mixers
        mp = [mod[i, bs:bs + 1, t] for t in range(6)]
        ms = [mod[i, :bs, t] for t in range(6)]
        if kind == 0:
            w_qkv = attn_w_qkv[j].astype(BF16)
            w_o = attn_w_o[j].astype(BF16)
            head_norms = jnp.stack([attn_q_norm[j], attn_k_norm[j]]).reshape(2, 1, HEAD_DIM)
            kv_w = N_KV_HEADS * HEAD_DIM
            xp, qkv_p = _attn_mixer(xp, streams["p"], mp[:3], norm_mix[i], w_qkv, head_norms, w_o, None, None, None)
            xs, _ = _attn_mixer(xs, streams["s"], ms[:3], norm_mix[i], w_qkv, head_norms, w_o, rope,
                                cache_k[:, j], cache_v[:, j])
            q_w = qkv_p.shape[-1] - 2 * kv_w
            new_k.append(qkv_p[:, :, q_w:q_w + kv_w].reshape(bp, sp, N_KV_HEADS, HEAD_DIM))
            new_v.append(qkv_p[:, :, q_w + kv_w:].reshape(bp, sp, N_KV_HEADS, HEAD_DIM))
        elif kind == 1:
            w_in = ssd_w_in[j].astype(BF16)
            w_in = jnp.pad(w_in, ((0, 0), (0, -w_in.shape[1] % _SSD_IN_TN)))
            w_out = ssd_w_out[j].astype(BF16)
            d_skip_row = jnp.repeat(ssd_d[j], ssd_p).reshape(SSD_GROUPS, 1, _GROUP_W)
            args = (norm_mix[i], w_in, ssd_conv_w[j], ssd_conv_b[j], ssd_dt_bias[j], ssd_a_log[j], d_skip_row,
                    ssd_norm[j], w_out)
            zero = jnp.zeros((bp, 2, SSD_GROUPS, SSD_STATE, _GROUP_W), F32)
            h0 = jnp.stack([_states_to_kernel_layout(state_ssm[:, j, 0]),
                            _states_to_kernel_layout(state_ssm[:, j, 1])], axis=1)
            xp, h_t = _ssd_mixer(xp, streams["p"], mp[:3], *args, zero)
            xs, _ = _ssd_mixer(xs, streams["s"], ms[:3], *args, h0)
            new_ssm.append(_states_from_kernel_layout(h_t, ssd_nh, ssd_p))
        else:
            w_in = lru_w_in[j].astype(BF16)
            w_out = lru_w_out[j].astype(BF16)
            width = w_out.shape[0]
            w_gate = jnp.concatenate([lru_w_a[j], lru_w_i[j]], axis=-1).astype(BF16)
            args = (norm_mix[i], w_in, lru_conv_w[j], lru_conv_b[j], w_gate, lru_b_a[j].reshape(2, 1, width),
                    lru_b_i[j].reshape(2, 1, width), lru_a_param[j].reshape(2, 1, width), w_out)
            xp, h_t = _lru_mixer(xp, streams["p"], mp[:3], *args, jnp.zeros((bp, 2, 1, width), F32))
            xs, _ = _lru_mixer(xs, streams["s"], ms[:3], *args, state_lru[:, j].reshape(bs, 2, 1, width))
            new_lru.append(h_t.reshape(bp, 2, width))
        w1 = ffn_w_in[i].astype(BF16)
        w2 = ffn_w_out[i].astype(BF16)
        xp = _ffn(xp, norm_ffn[i], mp[3], mp[4], mp[5], w1, w2)
        xs = _ffn(xs, norm_ffn[i], ms[3], ms[4], ms[5], w1, w2)
    y_prompt = _final_norm(xp, final_norm).reshape(bp, sp, d)
    y_sample = _final_norm(xs, final_norm)
    return (y_prompt, y_sample, jnp.stack(new_k, axis=1), jnp.stack(new_v, axis=1),
            jnp.stack(new_ssm, axis=1), jnp.stack(new_lru, axis=1))
```

```python
import functools
import math

import jax
import jax.numpy as jnp
from jax import lax
from jax.experimental import pallas as pl
from jax.experimental.pallas import tpu as pltpu

F32 = jnp.float32
BF16 = jnp.bfloat16
EPS = 1e-6

GRID_W = 64
HEAD_DIM = 128
N_KV_HEADS = 4
Q_PER_KV = 4
ROPE_THETA = 10000.0
SSD_HEAD_DIM = 64
SSD_GROUPS = 8
SSD_STATE = 128
SSD_CHUNK = 128
LRU_BLOCK_W = 128
LRU_C = 8.0

LANES = 128
SUBLANES = 8
VMEM_LIMIT_BYTES = 56 * 1024 * 1024
VMEM_LIMIT_FFN_BYTES = 58 * 1024 * 1024


def _params(*sem, vmem=VMEM_LIMIT_BYTES):
    return pltpu.CompilerParams(dimension_semantics=sem, vmem_limit_bytes=vmem)


def _sigmoid(x):
    return 1.0 / (1.0 + jnp.exp(-x))


def _silu(x):
    return x * _sigmoid(x)


def _softplus(x):
    return jnp.maximum(x, 0.0) + jnp.log(1.0 + jnp.exp(-jnp.abs(x)))


def _dot(a, b):
    return jnp.dot(a, b, preferred_element_type=F32)


def _split3(v):
    v1 = v.astype(BF16)
    r1 = v - v1.astype(F32)
    v2 = r1.astype(BF16)
    v3 = (r1 - v2.astype(F32)).astype(BF16)
    return v1, v2, v3


def _adaln_kernel(c_ref, w_ref, b_ref, o_ref):
    s = _silu(c_ref[...]).astype(BF16)
    o_ref[0] = _dot(s, w_ref[0].astype(BF16)) + b_ref[0]


def _adaln(cond, ada_w, ada_b):
    depth, d, n = ada_w.shape
    rows = cond.shape[0]
    tn = 1024
    return pl.pallas_call(
        _adaln_kernel,
        grid=(depth, n // tn),
        in_specs=[pl.BlockSpec((rows, d), lambda l, j: (0, 0)),
                  pl.BlockSpec((1, d, tn), lambda l, j: (l, 0, j)),
                  pl.BlockSpec((1, 1, tn), lambda l, j: (l, 0, j))],
        out_specs=pl.BlockSpec((1, rows, tn), lambda l, j: (l, 0, j)),
        out_shape=jax.ShapeDtypeStruct((depth, rows, n), F32),
        compiler_params=_params("parallel", "arbitrary"),
        name="adaln",
    )(cond, ada_w, ada_b.reshape(depth, 1, n))


def _norm_modulate(x, nw, shift, scale):
    y = x * lax.rsqrt(jnp.mean(x * x, axis=-1, keepdims=True) + EPS)
    return ((y * nw) * (1.0 + scale) + shift).astype(BF16)


_ROW_CHUNK = 256


def _row_chunks(tm):
    rc = min(_ROW_CHUNK, tm)
    return [slice(r, r + rc) for r in range(0, tm, rc)]


def _nmm_kernel(x_ref, nw_ref, sh_ref, sc_ref, w_ref, o_ref, h_ref):
    j = pl.program_id(2)

    @pl.when(j == 0)
    def _():
        for rows in _row_chunks(h_ref.shape[0]):
            h = _norm_modulate(x_ref[0, rows, :], nw_ref[...], sh_ref[0], sc_ref[0])
            h_ref[rows, :] = h
            o_ref[0, rows, :] = _dot(h, w_ref[...]).astype(o_ref.dtype)

    @pl.when(j > 0)
    def _():
        o_ref[0] = _dot(h_ref[...], w_ref[...]).astype(o_ref.dtype)


def _norm_mod_matmul(x, nw, shift, scale, w, *, tm=1024, tn=512, out_dtype=F32, name="nmm"):
    nb, l, d = x.shape
    n = w.shape[1]
    tm = min(tm, l)
    return pl.pallas_call(
        _nmm_kernel,
        grid=(nb, l // tm, n // tn),
        in_specs=[pl.BlockSpec((1, tm, d), lambda b, i, j: (b, i, 0)),
                  pl.BlockSpec((1, d), lambda b, i, j: (0, 0)),
                  pl.BlockSpec((1, 1, d), lambda b, i, j: (b, 0, 0)),
                  pl.BlockSpec((1, 1, d), lambda b, i, j: (b, 0, 0)),
                  pl.BlockSpec((d, tn), lambda b, i, j: (0, j))],
        out_specs=pl.BlockSpec((1, tm, tn), lambda b, i, j: (b, i, j)),
        out_shape=jax.ShapeDtypeStruct((nb, l, n), out_dtype),
        scratch_shapes=[pltpu.VMEM((tm, d), BF16)],
        compiler_params=_params("parallel", "parallel", "arbitrary"),
        name=name,
    )(x, nw.reshape(1, d), shift, scale, w)


_QKV_TILE_HEADS = 2 * N_KV_HEADS
_QKV_TN = _QKV_TILE_HEADS * HEAD_DIM
_KV_TILE = (N_KV_HEADS * Q_PER_KV) // _QKV_TILE_HEADS


def _qkv_kernel(x_ref, nw_ref, sh_ref, sc_ref, w_ref, hn_ref, cos_ref, sin_ref, o_ref, h_ref, *, rope):
    j = pl.program_id(2)
    is_kv_tile = j == _KV_TILE
    head_norm = jnp.where(is_kv_tile, hn_ref[1], hn_ref[0])

    def emit(rows, h):
        acc = _dot(h, w_ref[...])
        for hh in range(_QKV_TILE_HEADS):
            cols = slice(hh * HEAD_DIM, (hh + 1) * HEAD_DIM)
            xh = acc[:, cols]
            xn = xh * lax.rsqrt(jnp.mean(xh * xh, axis=-1, keepdims=True) + EPS) * head_norm
            if rope:
                xn = xn * cos_ref[rows, :] + pltpu.roll(xn, HEAD_DIM // 2, 1) * sin_ref[rows, :]
            if hh >= N_KV_HEADS:
                xn = jnp.where(is_kv_tile, xh, xn)
            o_ref[0, rows, cols] = xn.astype(o_ref.dtype)

    @pl.when(j == 0)
    def _():
        for rows in _row_chunks(h_ref.shape[0]):
            h = _norm_modulate(x_ref[0, rows, :], nw_ref[...], sh_ref[0], sc_ref[0])
            h_ref[rows, :] = h
            emit(rows, h)

    @pl.when(j > 0)
    def _():
        for rows in _row_chunks(h_ref.shape[0]):
            emit(rows, h_ref[rows, :])


def _qkv_proj(x, nw, shift, scale, w, head_norms, cos_t, sin_t, *, rope, out_dtype, tm=1024):
    nb, l, d = x.shape
    n = w.shape[1]
    tm = min(tm, l)
    tn = _QKV_TN
    return pl.pallas_call(
        functools.partial(_qkv_kernel, rope=rope),
        grid=(nb, l // tm, n // tn),
        in_specs=[pl.BlockSpec((1, tm, d), lambda b, i, j: (b, i, 0)),
                  pl.BlockSpec((1, d), lambda b, i, j: (0, 0)),
                  pl.BlockSpec((1, 1, d), lambda b, i, j: (b, 0, 0)),
                  pl.BlockSpec((1, 1, d), lambda b, i, j: (b, 0, 0)),
                  pl.BlockSpec((d, tn), lambda b, i, j: (0, j)),
                  pl.BlockSpec((2, 1, HEAD_DIM), lambda b, i, j: (0, 0, 0)),
                  pl.BlockSpec((tm, HEAD_DIM), lambda b, i, j: (i, 0)),
                  pl.BlockSpec((tm, HEAD_DIM), lambda b, i, j: (i, 0))],
        out_specs=pl.BlockSpec((1, tm, tn), lambda b, i, j: (b, i, j)),
        out_shape=jax.ShapeDtypeStruct((nb, l, n), out_dtype),
        scratch_shapes=[pltpu.VMEM((tm, d), BF16)],
        compiler_params=_params("parallel", "parallel", "arbitrary"),
        name="qkv_proj",
    )(x, nw.reshape(1, d), shift, scale, w, head_norms, cos_t, sin_t)


_ATTN_KEY_BLOCK = 256


def _attn_kernel(q_ref, k_ref, vt_ref, o_ref):
    tq = q_ref.shape[1]
    cols = Q_PER_KV * tq
    n_blocks, _, tkb = vt_ref.shape[2:]
    q = q_ref[0].astype(F32) * (HEAD_DIM ** -0.5 * math.log2(math.e))
    q4 = jnp.concatenate([q[:, g * HEAD_DIM:(g + 1) * HEAD_DIM] for g in range(Q_PER_KV)], axis=0)
    qt = q4.T.astype(BF16)

    def scores(kb):
        return _dot(k_ref[0, 0, pl.ds(pl.multiple_of(kb * tkb, tkb), tkb), :], qt)

    def update(kb, s, m, l, acc):
        m_new = jnp.maximum(m, jnp.max(s, axis=0, keepdims=True))
        alpha = jnp.exp2(m - m_new)
        p = jnp.exp2(s - m_new)
        l = alpha * l + jnp.sum(p, axis=0, keepdims=True)
        acc = alpha * acc + _dot(vt_ref[0, 0, kb], p.astype(BF16))
        return m_new, l, acc

    unroll = next(u for u in (9, 6, 3, 2, 1) if n_blocks % u == 0)

    def body(i, carry):
        ss = [scores(i * unroll + u) for u in range(unroll)]
        for u in range(unroll):
            carry = update(i * unroll + u, ss[u], *carry)
        return carry

    init = (jnp.full((1, cols), -jnp.inf, F32), jnp.zeros((1, cols), F32), jnp.zeros((HEAD_DIM, cols), F32))
    _, l, acc = lax.fori_loop(0, n_blocks // unroll, body, init)
    o = (acc / l).T
    for g in range(Q_PER_KV):
        o_ref[0, :, g * HEAD_DIM:(g + 1) * HEAD_DIM] = o[g * tq:(g + 1) * tq].astype(o_ref.dtype)


def _attention(qkv, k_ctx=None, v_ctx=None, *, tq=256):
    b, l, _ = qkv.shape
    tq = min(tq, l)
    q_w = N_KV_HEADS * Q_PER_KV * HEAD_DIM
    kv_w = N_KV_HEADS * HEAD_DIM
    k = qkv[:, :, q_w:q_w + kv_w].astype(BF16).reshape(b, l, N_KV_HEADS, HEAD_DIM)
    v = qkv[:, :, q_w + kv_w:].astype(BF16).reshape(b, l, N_KV_HEADS, HEAD_DIM)
    if k_ctx is not None:
        k = jnp.concatenate([k, k_ctx.astype(BF16)], axis=1)
        v = jnp.concatenate([v, v_ctx.astype(BF16)], axis=1)
    lk = k.shape[1]
    tkb = _ATTN_KEY_BLOCK
    assert lk % tkb == 0, (lk, tkb)
    n_blocks = lk // tkb
    k = k.transpose(0, 2, 1, 3)
    vt = v.reshape(b, n_blocks, tkb, N_KV_HEADS, HEAD_DIM).transpose(0, 3, 1, 4, 2)
    return pl.pallas_call(
        _attn_kernel,
        grid=(b, N_KV_HEADS, l // tq),
        in_specs=[pl.BlockSpec((1, tq, Q_PER_KV * HEAD_DIM), lambda bi, h, qi: (bi, qi, h)),
                  pl.BlockSpec((1, 1, lk, HEAD_DIM), lambda bi, h, qi: (bi, h, 0, 0)),
                  pl.BlockSpec((1, 1, n_blocks, HEAD_DIM, tkb), lambda bi, h, qi: (bi, h, 0, 0, 0))],
        out_specs=pl.BlockSpec((1, tq, Q_PER_KV * HEAD_DIM), lambda bi, h, qi: (bi, qi, h)),
        out_shape=jax.ShapeDtypeStruct((b, l, q_w), BF16),
        compiler_params=_params("parallel", "parallel", "arbitrary"),
        name="attention",
    )(qkv, k, vt)


def _plain_out_kernel(a_ref, w_ref, x_ref, g_ref, o_ref):
    o_ref[0] = x_ref[0] + g_ref[0] * _dot(a_ref[0], w_ref[...])


def _lru_out_kernel(y_ref, gb_ref, w_ref, x_ref, g_ref, o_ref, h_ref):
    j = pl.program_id(2)

    @pl.when(j == 0)
    def _():
        for rows in _row_chunks(h_ref.shape[0]):
            h = (y_ref[0, rows, :] * jax.nn.gelu(gb_ref[0, rows, :], approximate=True)).astype(BF16)
            h_ref[rows, :] = h
            o_ref[0, rows, :] = x_ref[0, rows, :] + g_ref[0] * _dot(h, w_ref[...])

    @pl.when(j > 0)
    def _():
        o_ref[0] = x_ref[0] + g_ref[0] * _dot(h_ref[...], w_ref[...])


def _mixer_out(kind, acts, w, x, gate, *, tm=1024, tn=1024):
    nb, l, d = x.shape
    k = w.shape[0]
    tm = min(tm, l)
    act_specs = [pl.BlockSpec((1, tm, k), lambda b, i, j: (b, i, 0)) for _ in acts]
    tail_specs = [pl.BlockSpec((k, tn), lambda b, i, j: (0, j)),
                  pl.BlockSpec((1, tm, tn), lambda b, i, j: (b, i, j)),
                  pl.BlockSpec((1, 1, tn), lambda b, i, j: (b, 0, j))]
    if kind == "lru":
        body, scratch = _lru_out_kernel, [pltpu.VMEM((tm, k), BF16)]
    else:
        body, scratch = _plain_out_kernel, []
    return pl.pallas_call(
        body,
        grid=(nb, l // tm, d // tn),
        in_specs=act_specs + tail_specs,
        out_specs=pl.BlockSpec((1, tm, tn), lambda b, i, j: (b, i, j)),
        out_shape=jax.ShapeDtypeStruct((nb, l, d), F32),
        scratch_shapes=scratch,
        compiler_params=_params("parallel", "parallel", "arbitrary"),
        name=kind + "_out",
    )(*acts, w, x, gate)


def _gated_norm_kernel(y_ref, z_ref, nw_ref, o_ref):
    t = y_ref[0] * _silu(z_ref[0])
    t = t * lax.rsqrt(jnp.mean(t * t, axis=-1, keepdims=True) + EPS)
    o_ref[0] = (t * nw_ref[...]).astype(o_ref.dtype)


def _gated_norm(y, z_src, nw, *, tm=512):
    nb, l, k = y.shape
    tm = min(tm, l)
    return pl.pallas_call(
        _gated_norm_kernel,
        grid=(nb, l // tm),
        in_specs=[pl.BlockSpec((1, tm, k), lambda b, i: (b, i, 0)),
                  pl.BlockSpec((1, tm, k), lambda b, i: (b, i, 0)),
                  pl.BlockSpec((1, k), lambda b, i: (0, 0))],
        out_specs=pl.BlockSpec((1, tm, k), lambda b, i: (b, i, 0)),
        out_shape=jax.ShapeDtypeStruct((nb, l, k), BF16),
        compiler_params=_params("parallel", "parallel"),
        name="gated_norm",
    )(y, z_src, nw.reshape(1, k))


def _ffn_kernel(x_ref, nw_ref, sh_ref, sc_ref, g_ref, wg_ref, wu_ref, wo_ref, fin_ref, o_ref, h_ref, *, final_norm):
    j = pl.program_id(2)

    def partial_out(h):
        a = (_silu(_dot(h, wg_ref[...])) * _dot(h, wu_ref[...])).astype(BF16)
        return _dot(a, wo_ref[...])

    @pl.when(j == 0)
    def _():
        for rows in _row_chunks(h_ref.shape[0]):
            h = _norm_modulate(x_ref[0, rows, :], nw_ref[...], sh_ref[0], sc_ref[0])
            h_ref[rows, :] = h
            o_ref[0, rows, :] = partial_out(h)

    @pl.when(j > 0)
    def _():
        o_ref[0] += partial_out(h_ref[...])

    @pl.when(j == pl.num_programs(2) - 1)
    def _():
        for rows in _row_chunks(h_ref.shape[0]):
            y = x_ref[0, rows, :] + g_ref[0] * o_ref[0, rows, :]
            if final_norm:
                y = y * lax.rsqrt(jnp.mean(y * y, axis=-1, keepdims=True) + EPS) * fin_ref[...]
            o_ref[0, rows, :] = y


def _ffn(x, nw, shift, scale, gate, w_in, w_out, fin_w, *, final_norm, tm=1024, tf=512):
    nb, l, d = x.shape
    f = w_out.shape[0]
    tm = min(tm, l)
    nf = f // tf
    x_mode = dict(pipeline_mode=pl.Buffered(1)) if final_norm else {}
    return pl.pallas_call(
        functools.partial(_ffn_kernel, final_norm=final_norm),
        grid=(nb, l // tm, nf),
        in_specs=[pl.BlockSpec((1, tm, d), lambda b, i, j: (b, i, 0), **x_mode),
                  pl.BlockSpec((1, d), lambda b, i, j: (0, 0)),
                  pl.BlockSpec((1, 1, d), lambda b, i, j: (b, 0, 0)),
                  pl.BlockSpec((1, 1, d), lambda b, i, j: (b, 0, 0)),
                  pl.BlockSpec((1, 1, d), lambda b, i, j: (b, 0, 0)),
                  pl.BlockSpec((d, tf), lambda b, i, j: (0, j)),
                  pl.BlockSpec((d, tf), lambda b, i, j: (0, nf + j)),
                  pl.BlockSpec((tf, d), lambda b, i, j: (j, 0)),
                  pl.BlockSpec((1, d), lambda b, i, j: (0, 0))],
        out_specs=pl.BlockSpec((1, tm, d), lambda b, i, j: (b, i, 0)),
        out_shape=jax.ShapeDtypeStruct((nb, l, d), F32),
        scratch_shapes=[pltpu.VMEM((tm, d), BF16)],
        compiler_params=_params("parallel", "parallel", "arbitrary", vmem=VMEM_LIMIT_FFN_BYTES),
        name="ffn",
    )(x, nw.reshape(1, d), shift, scale, gate, w_in, w_in, w_out, fin_w.reshape(1, d))


def _conv_kernel(prev_ref, cur_ref, nxt_ref, w_ref, b_ref, o_ref, buf_ref, *, silu):
    i = pl.program_id(1)
    tl = cur_ref.shape[1]
    h = SUBLANES
    cur = cur_ref[0]
    buf_ref[0:h] = jnp.where(i > 0, prev_ref[0], 0.0)
    buf_ref[h:h + tl] = cur
    buf_ref[h + tl:h + tl + h] = jnp.where(i < pl.num_programs(1) - 1, nxt_ref[0], 0.0)
    w = w_ref[...]
    y = (buf_ref[h - 1:h - 1 + tl] * w[0:1] + cur * w[1:2] + buf_ref[h + 1:h + 1 + tl] * w[2:3]
         + buf_ref[h + 2:h + 2 + tl] * w[3:4] + b_ref[...])
    o_ref[0] = _silu(y) if silu else y


def _dwconv(src, col0, width, w, bias, *, silu, tl=1024, ct=1024):
    nb, l, _ = src.shape
    tl = min(tl, l)
    c0 = col0 // ct
    rb = tl // SUBLANES
    last = l // SUBLANES - 1
    return pl.pallas_call(
        functools.partial(_conv_kernel, silu=silu),
        grid=(nb, l // tl, width // ct),
        in_specs=[pl.BlockSpec((1, SUBLANES, ct), lambda b, i, j: (b, jnp.maximum(i * rb - 1, 0), c0 + j)),
                  pl.BlockSpec((1, tl, ct), lambda b, i, j: (b, i, c0 + j)),
                  pl.BlockSpec((1, SUBLANES, ct), lambda b, i, j: (b, jnp.minimum((i + 1) * rb, last), c0 + j)),
                  pl.BlockSpec((4, ct), lambda b, i, j: (0, j)),
                  pl.BlockSpec((1, ct), lambda b, i, j: (0, j))],
        out_specs=pl.BlockSpec((1, tl, ct), lambda b, i, j: (b, i, j)),
        out_shape=jax.ShapeDtypeStruct((nb, l, width), F32),
        scratch_shapes=[pltpu.VMEM((tl + 2 * SUBLANES, ct), F32)],
        compiler_params=_params("parallel", "parallel", "parallel"),
        name="dwconv",
    )(src, src, src, w, bias.reshape(1, width))


def _ssd_dt_kernel(raw_ref, bias_ref, alog_ref, dt_ref, cs_ref):
    n = SSD_CHUNK
    dt = _softplus(raw_ref[0] + bias_ref[...])
    da = dt * (-jnp.exp(alog_ref[...]))
    ii = lax.broadcasted_iota(jnp.int32, (n, n), 0)
    jj = lax.broadcasted_iota(jnp.int32, (n, n), 1)
    lower = (ii >= jj).astype(BF16)
    upper = (ii <= jj).astype(BF16)
    parts = _split3(da)
    pre = sum(_dot(lower, p) for p in parts)
    suf = sum(_dot(upper, p) for p in parts)
    lane = lax.broadcasted_iota(jnp.int32, pre.shape, 1)
    dt_ref[0] = dt
    cs_ref[0] = jnp.where(lane < pre.shape[1] // 2, pre, suf)


def _ssd_dt(proj, col0, dt_bias, a_log):
    nb, l, _ = proj.shape
    nh2 = dt_bias.size
    out = jax.ShapeDtypeStruct((nb, l, nh2), F32)
    return pl.pallas_call(
        _ssd_dt_kernel,
        grid=(nb, l // SSD_CHUNK),
        in_specs=[pl.BlockSpec((1, SSD_CHUNK, nh2), lambda b, c: (b, c, col0 // nh2)),
                  pl.BlockSpec((1, nh2), lambda b, c: (0, 0)),
                  pl.BlockSpec((1, nh2), lambda b, c: (0, 0))],
        out_specs=[pl.BlockSpec((1, SSD_CHUNK, nh2), lambda b, c: (b, c, 0)),
                   pl.BlockSpec((1, SSD_CHUNK, nh2), lambda b, c: (b, c, 0))],
        out_shape=[out, out],
        compiler_params=_params("parallel", "parallel"),
        name="ssd_dt",
    )(proj, dt_bias.reshape(1, nh2), a_log.reshape(1, nh2))


_GROUP_W = 512
_HEADS_PER_GROUP = 8
_PAIR_W = 2 * SSD_HEAD_DIM
_SSD_GROUPS_PER_STEP = 2
_SSD_IN_TN = 1536


def _ssd_direction(d, g, gi, x_ref, b_ref, c_ref, dt_ref, cs_ref, cst_ref, dsk_ref, st_ref):
    n = SSD_CHUNK
    x = x_ref[0, :, gi * _GROUP_W:(gi + 1) * _GROUP_W]
    bm = b_ref[0, :, gi * SSD_STATE:(gi + 1) * SSD_STATE]
    cm = c_ref[0, :, gi * SSD_STATE:(gi + 1) * SSD_STATE].astype(BF16)
    dsk = dsk_ref[0, :, gi * _GROUP_W:(gi + 1) * _GROUP_W]
    n_heads = cs_ref.shape[2] // 2
    hbase = d * n_heads + g * _HEADS_PER_GROUP
    csg = pltpu.roll(cs_ref[0], (LANES - hbase) % LANES, 1)
    bt = bm.T
    scores = _dot(cm, bt.astype(BF16))
    ii = lax.broadcasted_iota(jnp.int32, (n, n), 0)
    jj = lax.broadcasted_iota(jnp.int32, (n, n), 1)
    mask = (ii >= jj) if d == 0 else (ii <= jj)
    low = jj < SSD_HEAD_DIM
    edge = n - 1 if d == 0 else 0
    st = st_ref[d, gi]
    y_inter = _dot(cm, st.astype(BF16))
    slabs = []
    for p in range(_HEADS_PER_GROUP // 2):
        h0, h1 = 2 * p, 2 * p + 1
        sl = slice(p * _PAIR_W, (p + 1) * _PAIR_W)
        c0 = jnp.broadcast_to(csg[:, h0:h0 + 1], (n, n))
        c1 = jnp.broadcast_to(csg[:, h1:h1 + 1], (n, n))
        r0 = cst_ref[0, pl.ds(hbase + h0, 1), :]
        r1 = cst_ref[0, pl.ds(hbase + h1, 1), :]
        d0 = dt_ref[0, pl.ds(hbase + h0, 1), :]
        d1 = dt_ref[0, pl.ds(hbase + h1, 1), :]
        l0 = jnp.exp(jnp.where(mask, c0 - r0, -jnp.inf))
        l1 = jnp.exp(jnp.where(mask, c1 - r1, -jnp.inf))
        m = jnp.concatenate([scores * (l0 * d0), scores * (l1 * d1)], axis=1).astype(BF16)
        xs = x[:, sl]
        xb = jnp.concatenate([jnp.where(low, xs, 0.0), jnp.where(low, 0.0, xs)], axis=0).astype(BF16)
        css = jnp.where(low, c0, c1)
        y = _dot(m, xb) + y_inter[:, sl] * jnp.exp(css)
        if d == 0:
            y = y + xs * dsk[:, sl]
        slabs.append(y)
        w0 = d0 * jnp.exp(r0[:, edge:edge + 1] - r0)
        w1 = d1 * jnp.exp(r1[:, edge:edge + 1] - r1)
        bw = jnp.concatenate([bt * w0, bt * w1], axis=1).astype(BF16)
        st_ref[d, gi, :, sl] = st[:, sl] * jnp.exp(css[edge:edge + 1, :]) + _dot(bw, xb)
    return jnp.concatenate(slabs, axis=1)


def _ssd_kernel(xf_ref, bf_ref, cf_ref, dtf_ref, csf_ref, cstf_ref,
                xb_ref, bb_ref, cb_ref, dtb_ref, csb_ref, cstb_ref,
                dsk_ref, h0_ref, y_ref, ht_ref, st_ref):
    gs = _SSD_GROUPS_PER_STEP
    g0 = pl.program_id(1) * gs
    c = pl.program_id(2)
    nc = pl.num_programs(2)

    @pl.when(c == 0)
    def _():
        st_ref[...] = h0_ref[0]

    yf = jnp.concatenate([_ssd_direction(0, g0 + gi, gi, xf_ref, bf_ref, cf_ref, dtf_ref, csf_ref, cstf_ref,
                                         dsk_ref, st_ref) for gi in range(gs)], axis=1)
    yb = jnp.concatenate([_ssd_direction(1, g0 + gi, gi, xb_ref, bb_ref, cb_ref, dtb_ref, csb_ref, cstb_ref,
                                         dsk_ref, st_ref) for gi in range(gs)], axis=1)
    rows_f = pl.ds(pl.multiple_of(c * SSD_CHUNK, SSD_CHUNK), SSD_CHUNK)
    rows_b = pl.ds(pl.multiple_of((nc - 1 - c) * SSD_CHUNK, SSD_CHUNK), SSD_CHUNK)

    @pl.when(c < nc // 2)
    def _():
        y_ref[0, rows_f, :] = yf
        y_ref[0, rows_b, :] = yb

    @pl.when(c >= nc // 2)
    def _():
        y_ref[0, rows_f, :] += yf
        y_ref[0, rows_b, :] += yb

    @pl.when(c == nc - 1)
    def _():
        ht_ref[0] = st_ref[...]


def _ssd_scan(xbc, dtt, cs, cst, d_skip_row, h0):
    nb, l, _ = xbc.shape
    nc = l // SSD_CHUNK
    assert nc % 2 == 0
    n = SSD_CHUNK
    d_inner = SSD_GROUPS * _GROUP_W
    b_blk0 = d_inner // SSD_STATE
    c_blk0 = b_blk0 + SSD_GROUPS
    nh2 = cs.shape[2]
    gs = _SSD_GROUPS_PER_STEP
    assert SSD_GROUPS % gs == 0 and b_blk0 % gs == 0 and c_blk0 % gs == 0
    d_skip_row = d_skip_row.reshape(SSD_GROUPS // gs, 1, gs * _GROUP_W)

    def chunk_specs(which):
        return [pl.BlockSpec((1, n, gs * _GROUP_W), lambda b, g, c: (b, which(c), g)),
                pl.BlockSpec((1, n, gs * SSD_STATE), lambda b, g, c: (b, which(c), b_blk0 // gs + g)),
                pl.BlockSpec((1, n, gs * SSD_STATE), lambda b, g, c: (b, which(c), c_blk0 // gs + g)),
                pl.BlockSpec((1, nh2, n), lambda b, g, c: (b, 0, which(c))),
                pl.BlockSpec((1, n, nh2), lambda b, g, c: (b, which(c), 0)),
                pl.BlockSpec((1, nh2, n), lambda b, g, c: (b, 0, which(c)))]

    state_shape = jax.ShapeDtypeStruct(h0.shape, F32)
    return pl.pallas_call(
        _ssd_kernel,
        grid=(nb, SSD_GROUPS // gs, nc),
        in_specs=(chunk_specs(lambda c: c) + chunk_specs(lambda c: nc - 1 - c)
                  + [pl.BlockSpec((1, 1, gs * _GROUP_W), lambda b, g, c: (g, 0, 0)),
                     pl.BlockSpec((1, 2, gs, SSD_STATE, _GROUP_W), lambda b, g, c: (b, 0, g, 0, 0))]),
        out_specs=[pl.BlockSpec((1, l, gs * _GROUP_W), lambda b, g, c: (b, 0, g)),
                   pl.BlockSpec((1, 2, gs, SSD_STATE, _GROUP_W), lambda b, g, c: (b, 0, g, 0, 0))],
        out_shape=[jax.ShapeDtypeStruct((nb, l, d_inner), F32), state_shape],
        scratch_shapes=[pltpu.VMEM((2, gs, SSD_STATE, _GROUP_W), F32)],
        compiler_params=_params("parallel", "parallel", "arbitrary"),
        name="ssd_scan",
    )(xbc, xbc, xbc, dtt, cs, cst, xbc, xbc, xbc, dtt, cs, cst, d_skip_row, h0)


_LRU_CT = 512
_LRU_ROWS = 128


def _tile_scan(a, u, h_in, reverse):
    rows = lax.broadcasted_iota(jnp.int32, a.shape, 0)
    for s in (1, 2, 4):
        if reverse:
            keep = rows < SUBLANES - s
            amount = SUBLANES - s
        else:
            keep = rows >= s
            amount = s
        a_s = jnp.where(keep, pltpu.roll(a, amount, 0), 1.0)
        u_s = jnp.where(keep, pltpu.roll(u, amount, 0), 0.0)
        u = a * u_s + u
        a = a * a_s
    h = a * h_in + u
    return h, (h[0:1] if reverse else h[SUBLANES - 1:SUBLANES])


def _lru_kernel(rec_ref, w_ref, ba_ref, bi_ref, ap_ref, h0_ref, o_ref, ht_ref, a_scr, u_scr):
    l = rec_ref.shape[1]
    nch = l // _LRU_ROWS
    n_tiles = _LRU_ROWS // SUBLANES
    n_blk = _LRU_CT // LRU_BLOCK_W

    def gates(d, start):
        rec = rec_ref[0, pl.ds(start, _LRU_ROWS), :]
        rate = _softplus(-ap_ref[d]) * (-LRU_C * math.log2(math.e))
        for k in range(n_blk):
            sl = slice(k * LRU_BLOCK_W, (k + 1) * LRU_BLOCK_W)
            rk = rec[:, sl]
            pre = _dot(rk.astype(BF16), w_ref[d, k])
            r = _sigmoid(pre[:, :LRU_BLOCK_W] + ba_ref[d][:, sl])
            ig = _sigmoid(pre[:, LRU_BLOCK_W:] + bi_ref[d][:, sl])
            a = jnp.exp2(r * rate[:, sl])
            a_scr[d, :, sl] = a
            u_scr[d, :, sl] = jnp.sqrt(1.0 - a * a) * (ig * rk)

    def chunk(c, carry, accumulate):
        hf, hb = carry
        start_f = pl.multiple_of(c * _LRU_ROWS, _LRU_ROWS)
        start_b = pl.multiple_of((nch - 1 - c) * _LRU_ROWS, _LRU_ROWS)
        gates(0, start_f)
        gates(1, start_b)
        for t in range(n_tiles):
            tf = t * SUBLANES
            tb = (n_tiles - 1 - t) * SUBLANES
            hs_f, hf = _tile_scan(a_scr[0, tf:tf + SUBLANES, :], u_scr[0, tf:tf + SUBLANES, :], hf, False)
            hs_b, hb = _tile_scan(a_scr[1, tb:tb + SUBLANES, :], u_scr[1, tb:tb + SUBLANES, :], hb, True)
            rows_f = pl.ds(pl.multiple_of(start_f + tf, SUBLANES), SUBLANES)
            rows_b = pl.ds(pl.multiple_of(start_b + tb, SUBLANES), SUBLANES)
            if accumulate:
                o_ref[0, rows_f, :] += hs_f
                o_ref[0, rows_b, :] += hs_b
            else:
                o_ref[0, rows_f, :] = hs_f
                o_ref[0, rows_b, :] = hs_b
        return hf, hb

    carry = (h0_ref[0, 0], h0_ref[0, 1])
    carry = lax.fori_loop(0, nch // 2, lambda c, cr: chunk(c, cr, False), carry)
    hf, hb = lax.fori_loop(nch // 2, nch, lambda c, cr: chunk(c, cr, True), carry)
    ht_ref[0, 0] = hf
    ht_ref[0, 1] = hb


def _lru_scan(rec, w_gate, b_a, b_i, a_param, h0):
    nb, l, w = rec.shape
    assert (l // _LRU_ROWS) % 2 == 0
    n_blk = _LRU_CT // LRU_BLOCK_W
    vec = pl.BlockSpec((2, 1, _LRU_CT), lambda b, j: (0, 0, j))
    return pl.pallas_call(
        _lru_kernel,
        grid=(nb, w // _LRU_CT),
        in_specs=[pl.BlockSpec((1, l, _LRU_CT), lambda b, j: (b, 0, j)),
                  pl.BlockSpec((2, n_blk, LRU_BLOCK_W, 2 * LRU_BLOCK_W), lambda b, j: (0, j, 0, 0)),
                  vec, vec, vec,
                  pl.BlockSpec((1, 2, 1, _LRU_CT), lambda b, j: (b, 0, 0, j))],
        out_specs=[pl.BlockSpec((1, l, _LRU_CT), lambda b, j: (b, 0, j)),
                   pl.BlockSpec((1, 2, 1, _LRU_CT), lambda b, j: (b, 0, 0, j))],
        out_shape=[jax.ShapeDtypeStruct((nb, l, w), F32), jax.ShapeDtypeStruct((nb, 2, 1, w), F32)],
        scratch_shapes=[pltpu.VMEM((2, _LRU_ROWS, _LRU_CT), F32), pltpu.VMEM((2, _LRU_ROWS, _LRU_CT), F32)],
        compiler_params=_params("parallel", "parallel"),
        name="lru_scan",
    )(rec, w_gate, b_a, b_i, a_param, h0)


def _rope_tables(length):
    pos = jnp.arange(length)
    row = (pos // GRID_W).astype(F32)
    col = (pos % GRID_W).astype(F32)
    n_freq = HEAD_DIM // 4
    inv = ROPE_THETA ** (-jnp.arange(n_freq, dtype=F32) / n_freq)
    ang = jnp.concatenate([row[:, None] * inv, col[:, None] * inv], axis=-1)
    cos, sin = jnp.cos(ang), jnp.sin(ang)
    return jnp.concatenate([cos, cos], axis=-1), jnp.concatenate([-sin, sin], axis=-1)


def _attn_mixer(x, seq_shape, mods, nw, w_qkv, head_norms, w_o, rope, k_ctx, v_ctx):
    shift, scale, gate = mods
    nb, l, d = x.shape
    b, s = seq_shape
    use_rope = rope is not None
    cos_t, sin_t = rope if use_rope else (jnp.zeros((l, HEAD_DIM), F32), jnp.zeros((l, HEAD_DIM), F32))
    qkv = _qkv_proj(x, nw, shift, scale, w_qkv, head_norms, cos_t, sin_t, rope=use_rope,
                    out_dtype=BF16 if use_rope else F32)
    qkv_seq = qkv.reshape(b, s, qkv.shape[-1])
    o = _attention(qkv_seq, k_ctx, v_ctx)
    x = _mixer_out("attn", [o.reshape(nb, l, d)], w_o, x, gate)
    return x, qkv_seq


def _ssd_mixer(x, seq_shape, mods, nw, w_in, conv_w, conv_b, dt_bias, a_log, d_skip_row, norm_w, w_out, h0):
    shift, scale, gate = mods
    nb, l, d = x.shape
    b, s = seq_shape
    d_inner = w_out.shape[0]
    conv_dim = conv_w.shape[1]
    proj = _norm_mod_matmul(x, nw, shift, scale, w_in, tn=_SSD_IN_TN, name="ssd_in")
    proj_seq = proj.reshape(b, s, proj.shape[-1])
    xbc = _dwconv(proj_seq, d_inner, conv_dim, conv_w, conv_b, silu=True)
    dt, cs = _ssd_dt(proj_seq, d_inner + conv_dim, dt_bias, a_log)
    y, h_t = _ssd_scan(xbc, jnp.swapaxes(dt, 1, 2), cs, jnp.swapaxes(cs, 1, 2), d_skip_row, h0)
    hn = _gated_norm(y.reshape(nb, l, d_inner), proj, norm_w)
    x = _mixer_out("ssd", [hn], w_out, x, gate, tm=512)
    return x, h_t


def _lru_mixer(x, seq_shape, mods, nw, w_in, conv_w, conv_b, w_gate, b_a, b_i, a_param, w_out, h0):
    shift, scale, gate = mods
    nb, l, d = x.shape
    b, s = seq_shape
    width = w_out.shape[0]
    proj = _norm_mod_matmul(x, nw, shift, scale, w_in, tn=1024, name="lru_in")
    rec = _dwconv(proj.reshape(b, s, 2 * width), width, width, conv_w, conv_b, silu=False)
    hs, h_t = _lru_scan(rec, w_gate, b_a, b_i, a_param, h0)
    x = _mixer_out("lru", [hs.reshape(nb, l, width), proj], w_out, x, gate, tm=512, tn=d)
    return x, h_t


def _states_to_kernel_layout(h):
    b, nh, p, n = h.shape
    return h.reshape(b, SSD_GROUPS, nh // SSD_GROUPS, p, n).transpose(0, 1, 4, 2, 3).reshape(b, SSD_GROUPS, n, -1)


def _states_from_kernel_layout(h, nh, p):
    b, two, g, n, _ = h.shape
    return h.reshape(b, two, g, n, nh // g, p).transpose(0, 1, 2, 4, 5, 3).reshape(b, two, nh, p, n)


def kernel(x_prompt, x_sample, cache_k, cache_v, state_ssm, state_lru, c, c_ctx, ada_w, ada_b, norm_mix, norm_ffn, attn_w_qkv, attn_q_norm, attn_k_norm, attn_w_o, ssd_w_in, ssd_conv_w, ssd_conv_b, ssd_dt_bias, ssd_a_log, ssd_d, ssd_norm, ssd_w_out, lru_w_in, lru_conv_w, lru_conv_b, lru_w_a, lru_b_a, lru_w_i, lru_b_i, lru_a_param, lru_w_out, ffn_w_in, ffn_w_out, final_norm):
    bp, sp, d = x_prompt.shape
    bs, ss, _ = x_sample.shape
    depth = ada_w.shape[0]
    n_mixers = 3

    pad_rows = -(bs + 1) % SUBLANES
    cond = jnp.concatenate([c, c_ctx[None, :], jnp.zeros((pad_rows, d), F32)], axis=0)
    mod = _adaln(cond, ada_w, ada_b).reshape(depth, bs + 1 + pad_rows, 6, 1, d)

    xp = x_prompt.reshape(1, bp * sp, d)
    xs = x_sample
    streams = {"p": (bp, sp), "s": (bs, ss)}
    rope = _rope_tables(ss)

    ssd_nh = ssd_d.shape[1]
    ssd_p = state_ssm.shape[4]
    new_k, new_v, new_ssm, new_lru = [], [], [], []
    for i in range(depth):
        kind, j = i % n_mixers, i // n_mixers
        mp = [mod[i, bs:bs + 1, t] for t in range(6)]
        ms = [mod[i, :bs, t] for t in range(6)]
        if kind == 0:
            w_qkv = attn_w_qkv[j].astype(BF16)
            w_o = attn_w_o[j].astype(BF16)
            head_norms = jnp.stack([attn_q_norm[j], attn_k_norm[j]]).reshape(2, 1, HEAD_DIM)
            kv_w = N_KV_HEADS * HEAD_DIM
            xp, qkv_p = _attn_mixer(xp, streams["p"], mp[:3], norm_mix[i], w_qkv, head_norms, w_o, None, None, None)
            xs, _ = _attn_mixer(xs, streams["s"], ms[:3], norm_mix[i], w_qkv, head_norms, w_o, rope,
                                cache_k[:, j], cache_v[:, j])
            q_w = qkv_p.shape[-1] - 2 * kv_w
            new_k.append(qkv_p[:, :, q_w:q_w + kv_w].reshape(bp, sp, N_KV_HEADS, HEAD_DIM))
            new_v.append(qkv_p[:, :, q_w + kv_w:].reshape(bp, sp, N_KV_HEADS, HEAD_DIM))
        elif kind == 1:
            w_in = ssd_w_in[j].astype(BF16)
            w_in = jnp.pad(w_in, ((0, 0), (0, -w_in.shape[1] % _SSD_IN_TN)))
            w_out = ssd_w_out[j].astype(BF16)
            d_skip_row = jnp.repeat(ssd_d[j], ssd_p).reshape(SSD_GROUPS, 1, _GROUP_W)
            args = (norm_mix[i], w_in, ssd_conv_w[j], ssd_conv_b[j], ssd_dt_bias[j], ssd_a_log[j], d_skip_row,
                    ssd_norm[j], w_out)
            zero = jnp.zeros((bp, 2, SSD_GROUPS, SSD_STATE, _GROUP_W), F32)
            h0 = jnp.stack([_states_to_kernel_layout(state_ssm[:, j, 0]),
                            _states_to_kernel_layout(state_ssm[:, j, 1])], axis=1)
            xp, h_t = _ssd_mixer(xp, streams["p"], mp[:3], *args, zero)
            xs, _ = _ssd_mixer(xs, streams["s"], ms[:3], *args, h0)
            new_ssm.append(_states_from_kernel_layout(h_t, ssd_nh, ssd_p))
        else:
            w_in = lru_w_in[j].astype(BF16)
            w_out = lru_w_out[j].astype(BF16)
            width = w_out.shape[0]
            w_gate = jnp.concatenate([lru_w_a[j], lru_w_i[j]], axis=-1).astype(BF16)
            args = (norm_mix[i], w_in, lru_conv_w[j], lru_conv_b[j], w_gate, lru_b_a[j].reshape(2, 1, width),
                    lru_b_i[j].reshape(2, 1, width), lru_a_param[j].reshape(2, 1, width), w_out)
            xp, h_t = _lru_mixer(xp, streams["p"], mp[:3], *args, jnp.zeros((bp, 2, 1, width), F32))
            xs, _ = _lru_mixer(xs, streams["s"], ms[:3], *args, state_lru[:, j].reshape(bs, 2, 1, width))
            new_lru.append(h_t.reshape(bp, 2, width))
        w1 = ffn_w_in[i].astype(BF16)
        w2 = ffn_w_out[i].astype(BF16)
        last = i == depth - 1
        xp = _ffn(xp, norm_ffn[i], mp[3], mp[4], mp[5], w1, w2, final_norm, final_norm=last)
        xs = _ffn(xs, norm_ffn[i], ms[3], ms[4], ms[5], w1, w2, final_norm, final_norm=last)
    y_prompt = xp.reshape(bp, sp, d)
    y_sample = xs
    return (y_prompt, y_sample, jnp.stack(new_k, axis=1), jnp.stack(new_v, axis=1),
            jnp.stack(new_ssm, axis=1), jnp.stack(new_lru, axis=1))
```

```python
import functools
import math

import jax
import jax.numpy as jnp
from jax import lax
from jax.experimental import pallas as pl
from jax.experimental.pallas import tpu as pltpu

F32 = jnp.float32
BF16 = jnp.bfloat16
EPS = 1e-6

GRID_W = 64
HEAD_DIM = 128
N_KV_HEADS = 4
Q_PER_KV = 4
ROPE_THETA = 10000.0
SSD_HEAD_DIM = 64
SSD_GROUPS = 8
SSD_STATE = 128
SSD_CHUNK = 128
LRU_BLOCK_W = 128
LRU_C = 8.0

LANES = 128
SUBLANES = 8
VMEM_LIMIT_BYTES = 56 * 1024 * 1024
VMEM_LIMIT_FFN_BYTES = 58 * 1024 * 1024


def _params(*sem, vmem=VMEM_LIMIT_BYTES):
    return pltpu.CompilerParams(dimension_semantics=sem, vmem_limit_bytes=vmem)


def _sigmoid(x):
    return 1.0 / (1.0 + jnp.exp(-x))


def _silu(x):
    return x * _sigmoid(x)


def _softplus(x):
    return jnp.maximum(x, 0.0) + jnp.log(1.0 + jnp.exp(-jnp.abs(x)))


def _dot(a, b):
    return jnp.dot(a, b, preferred_element_type=F32)


def _split3(v):
    v1 = v.astype(BF16)
    r1 = v - v1.astype(F32)
    v2 = r1.astype(BF16)
    v3 = (r1 - v2.astype(F32)).astype(BF16)
    return v1, v2, v3


def _adaln_kernel(c_ref, w_ref, b_ref, o_ref):
    s = _silu(c_ref[...]).astype(BF16)
    o_ref[0] = _dot(s, w_ref[0].astype(BF16)) + b_ref[0]


def _adaln(cond, ada_w, ada_b):
    depth, d, n = ada_w.shape
    rows = cond.shape[0]
    tn = 1024
    return pl.pallas_call(
        _adaln_kernel,
        grid=(depth, n // tn),
        in_specs=[pl.BlockSpec((rows, d), lambda l, j: (0, 0)),
                  pl.BlockSpec((1, d, tn), lambda l, j: (l, 0, j)),
                  pl.BlockSpec((1, 1, tn), lambda l, j: (l, 0, j))],
        out_specs=pl.BlockSpec((1, rows, tn), lambda l, j: (l, 0, j)),
        out_shape=jax.ShapeDtypeStruct((depth, rows, n), F32),
        compiler_params=_params("parallel", "arbitrary"),
        name="adaln",
    )(cond, ada_w, ada_b.reshape(depth, 1, n))


def _norm_modulate(x, nw, shift, scale):
    y = x * lax.rsqrt(jnp.mean(x * x, axis=-1, keepdims=True) + EPS)
    return ((y * nw) * (1.0 + scale) + shift).astype(BF16)


_ROW_CHUNK = 256


_EPILOGUE_ROW_CHUNK = 64


def _row_chunks(tm, chunk=_ROW_CHUNK):
    rc = min(chunk, tm)
    return [slice(r, r + rc) for r in range(0, tm, rc)]


def _nmm_kernel(x_ref, nw_ref, sh_ref, sc_ref, w_ref, o_ref, h_ref):
    j = pl.program_id(2)

    @pl.when(j == 0)
    def _():
        for rows in _row_chunks(h_ref.shape[0]):
            h = _norm_modulate(x_ref[0, rows, :], nw_ref[...], sh_ref[0], sc_ref[0])
            h_ref[rows, :] = h
            o_ref[0, rows, :] = _dot(h, w_ref[...]).astype(o_ref.dtype)

    @pl.when(j > 0)
    def _():
        o_ref[0] = _dot(h_ref[...], w_ref[...]).astype(o_ref.dtype)


def _norm_mod_matmul(x, nw, shift, scale, w, *, tm=1024, tn=512, out_dtype=F32, name="nmm"):
    nb, l, d = x.shape
    n = w.shape[1]
    tm = min(tm, l)
    return pl.pallas_call(
        _nmm_kernel,
        grid=(nb, l // tm, n // tn),
        in_specs=[pl.BlockSpec((1, tm, d), lambda b, i, j: (b, i, 0)),
                  pl.BlockSpec((1, d), lambda b, i, j: (0, 0)),
                  pl.BlockSpec((1, 1, d), lambda b, i, j: (b, 0, 0)),
                  pl.BlockSpec((1, 1, d), lambda b, i, j: (b, 0, 0)),
                  pl.BlockSpec((d, tn), lambda b, i, j: (0, j))],
        out_specs=pl.BlockSpec((1, tm, tn), lambda b, i, j: (b, i, j)),
        out_shape=jax.ShapeDtypeStruct((nb, l, n), out_dtype),
        scratch_shapes=[pltpu.VMEM((tm, d), BF16)],
        compiler_params=_params("parallel", "parallel", "arbitrary"),
        name=name,
    )(x, nw.reshape(1, d), shift, scale, w)


_QKV_TILE_HEADS = 2 * N_KV_HEADS
_QKV_TN = _QKV_TILE_HEADS * HEAD_DIM
_KV_TILE = (N_KV_HEADS * Q_PER_KV) // _QKV_TILE_HEADS


def _qkv_kernel(x_ref, nw_ref, sh_ref, sc_ref, w_ref, hn_ref, cos_ref, sin_ref, o_ref, h_ref, *, rope):
    j = pl.program_id(2)
    is_kv_tile = j == _KV_TILE
    head_norm = jnp.where(is_kv_tile, hn_ref[1], hn_ref[0])

    def emit(rows, h):
        acc = _dot(h, w_ref[...])
        for hh in range(_QKV_TILE_HEADS):
            cols = slice(hh * HEAD_DIM, (hh + 1) * HEAD_DIM)
            xh = acc[:, cols]
            xn = xh * lax.rsqrt(jnp.mean(xh * xh, axis=-1, keepdims=True) + EPS) * head_norm
            if rope:
                xn = xn * cos_ref[rows, :] + pltpu.roll(xn, HEAD_DIM // 2, 1) * sin_ref[rows, :]
            if hh >= N_KV_HEADS:
                xn = jnp.where(is_kv_tile, xh, xn)
            o_ref[0, rows, cols] = xn.astype(o_ref.dtype)

    @pl.when(j == 0)
    def _():
        for rows in _row_chunks(h_ref.shape[0]):
            h = _norm_modulate(x_ref[0, rows, :], nw_ref[...], sh_ref[0], sc_ref[0])
            h_ref[rows, :] = h
            emit(rows, h)

    @pl.when(j > 0)
    def _():
        for rows in _row_chunks(h_ref.shape[0]):
            emit(rows, h_ref[rows, :])


def _qkv_proj(x, nw, shift, scale, w, head_norms, cos_t, sin_t, *, rope, out_dtype, tm=1024):
    nb, l, d = x.shape
    n = w.shape[1]
    tm = min(tm, l)
    tn = _QKV_TN
    return pl.pallas_call(
        functools.partial(_qkv_kernel, rope=rope),
        grid=(nb, l // tm, n // tn),
        in_specs=[pl.BlockSpec((1, tm, d), lambda b, i, j: (b, i, 0)),
                  pl.BlockSpec((1, d), lambda b, i, j: (0, 0)),
                  pl.BlockSpec((1, 1, d), lambda b, i, j: (b, 0, 0)),
                  pl.BlockSpec((1, 1, d), lambda b, i, j: (b, 0, 0)),
                  pl.BlockSpec((d, tn), lambda b, i, j: (0, j)),
                  pl.BlockSpec((2, 1, HEAD_DIM), lambda b, i, j: (0, 0, 0)),
                  pl.BlockSpec((tm, HEAD_DIM), lambda b, i, j: (i, 0)),
                  pl.BlockSpec((tm, HEAD_DIM), lambda b, i, j: (i, 0))],
        out_specs=pl.BlockSpec((1, tm, tn), lambda b, i, j: (b, i, j)),
        out_shape=jax.ShapeDtypeStruct((nb, l, n), out_dtype),
        scratch_shapes=[pltpu.VMEM((tm, d), BF16)],
        compiler_params=_params("parallel", "parallel", "arbitrary"),
        name="qkv_proj",
    )(x, nw.reshape(1, d), shift, scale, w, head_norms, cos_t, sin_t)


_ATTN_KEY_BLOCK = 256


def _attn_kernel(q_ref, k_ref, vt_ref, o_ref):
    tq = q_ref.shape[1]
    cols = Q_PER_KV * tq
    n_blocks, _, tkb = vt_ref.shape[2:]
    q = q_ref[0].astype(F32) * (HEAD_DIM ** -0.5 * math.log2(math.e))
    q4 = jnp.concatenate([q[:, g * HEAD_DIM:(g + 1) * HEAD_DIM] for g in range(Q_PER_KV)], axis=0)
    qt = q4.T.astype(BF16)

    def scores(kb):
        return _dot(k_ref[0, 0, pl.ds(pl.multiple_of(kb * tkb, tkb), tkb), :], qt)

    def update(kb, s, m, l, acc):
        m_new = jnp.maximum(m, jnp.max(s, axis=0, keepdims=True))
        alpha = jnp.exp2(m - m_new)
        p = jnp.exp2(s - m_new)
        l = alpha * l + jnp.sum(p, axis=0, keepdims=True)
        acc = alpha * acc + _dot(vt_ref[0, 0, kb], p.astype(BF16))
        return m_new, l, acc

    unroll = next(u for u in (9, 6, 3, 2, 1) if n_blocks % u == 0)

    def body(i, carry):
        ss = [scores(i * unroll + u) for u in range(unroll)]
        for u in range(unroll):
            carry = update(i * unroll + u, ss[u], *carry)
        return carry

    init = (jnp.full((1, cols), -jnp.inf, F32), jnp.zeros((1, cols), F32), jnp.zeros((HEAD_DIM, cols), F32))
    _, l, acc = lax.fori_loop(0, n_blocks // unroll, body, init)
    o = (acc / l).T
    for g in range(Q_PER_KV):
        o_ref[0, :, g * HEAD_DIM:(g + 1) * HEAD_DIM] = o[g * tq:(g + 1) * tq].astype(o_ref.dtype)


def _attention(qkv, k_ctx=None, v_ctx=None, *, tq=256):
    b, l, _ = qkv.shape
    tq = min(tq, l)
    q_w = N_KV_HEADS * Q_PER_KV * HEAD_DIM
    kv_w = N_KV_HEADS * HEAD_DIM
    k = qkv[:, :, q_w:q_w + kv_w].astype(BF16).reshape(b, l, N_KV_HEADS, HEAD_DIM)
    v = qkv[:, :, q_w + kv_w:].astype(BF16).reshape(b, l, N_KV_HEADS, HEAD_DIM)
    if k_ctx is not None:
        k = jnp.concatenate([k, k_ctx.astype(BF16)], axis=1)
        v = jnp.concatenate([v, v_ctx.astype(BF16)], axis=1)
    lk = k.shape[1]
    tkb = _ATTN_KEY_BLOCK
    assert lk % tkb == 0, (lk, tkb)
    n_blocks = lk // tkb
    k = k.transpose(0, 2, 1, 3)
    vt = v.reshape(b, n_blocks, tkb, N_KV_HEADS, HEAD_DIM).transpose(0, 3, 1, 4, 2)
    return pl.pallas_call(
        _attn_kernel,
        grid=(b, N_KV_HEADS, l // tq),
        in_specs=[pl.BlockSpec((1, tq, Q_PER_KV * HEAD_DIM), lambda bi, h, qi: (bi, qi, h)),
                  pl.BlockSpec((1, 1, lk, HEAD_DIM), lambda bi, h, qi: (bi, h, 0, 0)),
                  pl.BlockSpec((1, 1, n_blocks, HEAD_DIM, tkb), lambda bi, h, qi: (bi, h, 0, 0, 0))],
        out_specs=pl.BlockSpec((1, tq, Q_PER_KV * HEAD_DIM), lambda bi, h, qi: (bi, qi, h)),
        out_shape=jax.ShapeDtypeStruct((b, l, q_w), BF16),
        compiler_params=_params("parallel", "parallel", "arbitrary"),
        name="attention",
    )(qkv, k, vt)


def _plain_out_kernel(a_ref, w_ref, x_ref, g_ref, o_ref):
    o_ref[0] = x_ref[0] + g_ref[0] * _dot(a_ref[0], w_ref[...])


def _lru_out_kernel(y_ref, gb_ref, w_ref, x_ref, g_ref, o_ref, h_ref):
    j = pl.program_id(2)

    @pl.when(j == 0)
    def _():
        for rows in _row_chunks(h_ref.shape[0]):
            h = (y_ref[0, rows, :] * jax.nn.gelu(gb_ref[0, rows, :], approximate=True)).astype(BF16)
            h_ref[rows, :] = h
            o_ref[0, rows, :] = x_ref[0, rows, :] + g_ref[0] * _dot(h, w_ref[...])

    @pl.when(j > 0)
    def _():
        o_ref[0] = x_ref[0] + g_ref[0] * _dot(h_ref[...], w_ref[...])


def _mixer_out(kind, acts, w, x, gate, *, tm=1024, tn=1024):
    nb, l, d = x.shape
    k = w.shape[0]
    tm = min(tm, l)
    act_specs = [pl.BlockSpec((1, tm, k), lambda b, i, j: (b, i, 0)) for _ in acts]
    tail_specs = [pl.BlockSpec((k, tn), lambda b, i, j: (0, j)),
                  pl.BlockSpec((1, tm, tn), lambda b, i, j: (b, i, j)),
                  pl.BlockSpec((1, 1, tn), lambda b, i, j: (b, 0, j))]
    if kind == "lru":
        body, scratch = _lru_out_kernel, [pltpu.VMEM((tm, k), BF16)]
    else:
        body, scratch = _plain_out_kernel, []
    return pl.pallas_call(
        body,
        grid=(nb, l // tm, d // tn),
        in_specs=act_specs + tail_specs,
        out_specs=pl.BlockSpec((1, tm, tn), lambda b, i, j: (b, i, j)),
        out_shape=jax.ShapeDtypeStruct((nb, l, d), F32),
        scratch_shapes=scratch,
        compiler_params=_params("parallel", "parallel", "arbitrary"),
        name=kind + "_out",
    )(*acts, w, x, gate)


def _gated_norm_kernel(y_ref, z_ref, nw_ref, o_ref):
    t = y_ref[0] * _silu(z_ref[0])
    t = t * lax.rsqrt(jnp.mean(t * t, axis=-1, keepdims=True) + EPS)
    o_ref[0] = (t * nw_ref[...]).astype(o_ref.dtype)


def _gated_norm(y, z_src, nw, *, tm=512):
    nb, l, k = y.shape
    tm = min(tm, l)
    return pl.pallas_call(
        _gated_norm_kernel,
        grid=(nb, l // tm),
        in_specs=[pl.BlockSpec((1, tm, k), lambda b, i: (b, i, 0)),
                  pl.BlockSpec((1, tm, k), lambda b, i: (b, i, 0)),
                  pl.BlockSpec((1, k), lambda b, i: (0, 0))],
        out_specs=pl.BlockSpec((1, tm, k), lambda b, i: (b, i, 0)),
        out_shape=jax.ShapeDtypeStruct((nb, l, k), BF16),
        compiler_params=_params("parallel", "parallel"),
        name="gated_norm",
    )(y, z_src, nw.reshape(1, k))


def _ffn_kernel(x_ref, nw_ref, sh_ref, sc_ref, g_ref, wg_ref, wu_ref, wo_ref, fin_ref, o_ref, h_ref, *, final_norm):
    j = pl.program_id(2)

    def partial_out(h):
        a = (_silu(_dot(h, wg_ref[...])) * _dot(h, wu_ref[...])).astype(BF16)
        return _dot(a, wo_ref[...])

    @pl.when(j == 0)
    def _():
        for rows in _row_chunks(h_ref.shape[0]):
            h = _norm_modulate(x_ref[0, rows, :], nw_ref[...], sh_ref[0], sc_ref[0])
            h_ref[rows, :] = h
            o_ref[0, rows, :] = partial_out(h)

    @pl.when(j > 0)
    def _():
        o_ref[0] += partial_out(h_ref[...])

    @pl.when(j == pl.num_programs(2) - 1)
    def _():
        rc = min(_EPILOGUE_ROW_CHUNK, h_ref.shape[0])

        def finish(r, carry):
            rows = pl.ds(pl.multiple_of(r * rc, rc), rc)
            y = x_ref[0, rows, :] + g_ref[0] * o_ref[0, rows, :]
            if final_norm:
                y = y * lax.rsqrt(jnp.mean(y * y, axis=-1, keepdims=True) + EPS) * fin_ref[...]
            o_ref[0, rows, :] = y
            return carry

        lax.fori_loop(0, h_ref.shape[0] // rc, finish, 0)


def _ffn(x, nw, shift, scale, gate, w_in, w_out, fin_w, *, final_norm, tm=1024, tf=512):
    nb, l, d = x.shape
    f = w_out.shape[0]
    tm = min(tm, l)
    nf = f // tf
    return pl.pallas_call(
        functools.partial(_ffn_kernel, final_norm=final_norm),
        grid=(nb, l // tm, nf),
        in_specs=[pl.BlockSpec((1, tm, d), lambda b, i, j: (b, i, 0)),
                  pl.BlockSpec((1, d), lambda b, i, j: (0, 0)),
                  pl.BlockSpec((1, 1, d), lambda b, i, j: (b, 0, 0)),
                  pl.BlockSpec((1, 1, d), lambda b, i, j: (b, 0, 0)),
                  pl.BlockSpec((1, 1, d), lambda b, i, j: (b, 0, 0)),
                  pl.BlockSpec((d, tf), lambda b, i, j: (0, j)),
                  pl.BlockSpec((d, tf), lambda b, i, j: (0, nf + j)),
                  pl.BlockSpec((tf, d), lambda b, i, j: (j, 0)),
                  pl.BlockSpec((1, d), lambda b, i, j: (0, 0))],
        out_specs=pl.BlockSpec((1, tm, d), lambda b, i, j: (b, i, 0)),
        out_shape=jax.ShapeDtypeStruct((nb, l, d), F32),
        scratch_shapes=[pltpu.VMEM((tm, d), BF16)],
        compiler_params=_params("parallel", "parallel", "arbitrary", vmem=VMEM_LIMIT_FFN_BYTES),
        name="ffn",
    )(x, nw.reshape(1, d), shift, scale, gate, w_in, w_in, w_out, fin_w.reshape(1, d))


def _conv_kernel(prev_ref, cur_ref, nxt_ref, w_ref, b_ref, o_ref, buf_ref, *, silu):
    i = pl.program_id(1)
    tl = cur_ref.shape[1]
    h = SUBLANES
    cur = cur_ref[0]
    buf_ref[0:h] = jnp.where(i > 0, prev_ref[0], 0.0)
    buf_ref[h:h + tl] = cur
    buf_ref[h + tl:h + tl + h] = jnp.where(i < pl.num_programs(1) - 1, nxt_ref[0], 0.0)
    w = w_ref[...]
    y = (buf_ref[h - 1:h - 1 + tl] * w[0:1] + cur * w[1:2] + buf_ref[h + 1:h + 1 + tl] * w[2:3]
         + buf_ref[h + 2:h + 2 + tl] * w[3:4] + b_ref[...])
    o_ref[0] = _silu(y) if silu else y


def _dwconv(src, col0, width, w, bias, *, silu, tl=1024, ct=1024):
    nb, l, _ = src.shape
    tl = min(tl, l)
    c0 = col0 // ct
    rb = tl // SUBLANES
    last = l // SUBLANES - 1
    return pl.pallas_call(
        functools.partial(_conv_kernel, silu=silu),
        grid=(nb, l // tl, width // ct),
        in_specs=[pl.BlockSpec((1, SUBLANES, ct), lambda b, i, j: (b, jnp.maximum(i * rb - 1, 0), c0 + j)),
                  pl.BlockSpec((1, tl, ct), lambda b, i, j: (b, i, c0 + j)),
                  pl.BlockSpec((1, SUBLANES, ct), lambda b, i, j: (b, jnp.minimum((i + 1) * rb, last), c0 + j)),
                  pl.BlockSpec((4, ct), lambda b, i, j: (0, j)),
                  pl.BlockSpec((1, ct), lambda b, i, j: (0, j))],
        out_specs=pl.BlockSpec((1, tl, ct), lambda b, i, j: (b, i, j)),
        out_shape=jax.ShapeDtypeStruct((nb, l, width), F32),
        scratch_shapes=[pltpu.VMEM((tl + 2 * SUBLANES, ct), F32)],
        compiler_params=_params("parallel", "parallel", "parallel"),
        name="dwconv",
    )(src, src, src, w, bias.reshape(1, width))


def _ssd_dt_kernel(raw_ref, bias_ref, alog_ref, dt_ref, cs_ref):
    n = SSD_CHUNK
    dt = _softplus(raw_ref[0] + bias_ref[...])
    da = dt * (-jnp.exp(alog_ref[...]))
    ii = lax.broadcasted_iota(jnp.int32, (n, n), 0)
    jj = lax.broadcasted_iota(jnp.int32, (n, n), 1)
    lower = (ii >= jj).astype(BF16)
    upper = (ii <= jj).astype(BF16)
    parts = _split3(da)
    pre = sum(_dot(lower, p) for p in parts)
    suf = sum(_dot(upper, p) for p in parts)
    lane = lax.broadcasted_iota(jnp.int32, pre.shape, 1)
    dt_ref[0] = dt
    cs_ref[0] = jnp.where(lane < pre.shape[1] // 2, pre, suf)


def _ssd_dt(proj, col0, dt_bias, a_log):
    nb, l, _ = proj.shape
    nh2 = dt_bias.size
    out = jax.ShapeDtypeStruct((nb, l, nh2), F32)
    return pl.pallas_call(
        _ssd_dt_kernel,
        grid=(nb, l // SSD_CHUNK),
        in_specs=[pl.BlockSpec((1, SSD_CHUNK, nh2), lambda b, c: (b, c, col0 // nh2)),
                  pl.BlockSpec((1, nh2), lambda b, c: (0, 0)),
                  pl.BlockSpec((1, nh2), lambda b, c: (0, 0))],
        out_specs=[pl.BlockSpec((1, SSD_CHUNK, nh2), lambda b, c: (b, c, 0)),
                   pl.BlockSpec((1, SSD_CHUNK, nh2), lambda b, c: (b, c, 0))],
        out_shape=[out, out],
        compiler_params=_params("parallel", "parallel"),
        name="ssd_dt",
    )(proj, dt_bias.reshape(1, nh2), a_log.reshape(1, nh2))


_GROUP_W = 512
_HEADS_PER_GROUP = 8
_PAIR_W = 2 * SSD_HEAD_DIM
_SSD_GROUPS_PER_STEP = 2
_SSD_IN_TN = 1536


def _ssd_direction(d, g, gi, x_ref, b_ref, c_ref, dt_ref, cs_ref, cst_ref, dsk_ref, st_ref):
    n = SSD_CHUNK
    x = x_ref[0, :, gi * _GROUP_W:(gi + 1) * _GROUP_W]
    bm = b_ref[0, :, gi * SSD_STATE:(gi + 1) * SSD_STATE]
    cm = c_ref[0, :, gi * SSD_STATE:(gi + 1) * SSD_STATE].astype(BF16)
    dsk = dsk_ref[0, :, gi * _GROUP_W:(gi + 1) * _GROUP_W]
    n_heads = cs_ref.shape[2] // 2
    hbase = d * n_heads + g * _HEADS_PER_GROUP
    csg = pltpu.roll(cs_ref[0], (LANES - hbase) % LANES, 1)
    bt = bm.T
    scores = _dot(cm, bt.astype(BF16))
    ii = lax.broadcasted_iota(jnp.int32, (n, n), 0)
    jj = lax.broadcasted_iota(jnp.int32, (n, n), 1)
    mask = (ii >= jj) if d == 0 else (ii <= jj)
    low = jj < SSD_HEAD_DIM
    edge = n - 1 if d == 0 else 0
    st = st_ref[d, gi]
    y_inter = _dot(cm, st.astype(BF16))
    slabs = []
    for p in range(_HEADS_PER_GROUP // 2):
        h0, h1 = 2 * p, 2 * p + 1
        sl = slice(p * _PAIR_W, (p + 1) * _PAIR_W)
        c0 = jnp.broadcast_to(csg[:, h0:h0 + 1], (n, n))
        c1 = jnp.broadcast_to(csg[:, h1:h1 + 1], (n, n))
        r0 = cst_ref[0, pl.ds(hbase + h0, 1), :]
        r1 = cst_ref[0, pl.ds(hbase + h1, 1), :]
        d0 = dt_ref[0, pl.ds(hbase + h0, 1), :]
        d1 = dt_ref[0, pl.ds(hbase + h1, 1), :]
        l0 = jnp.exp(jnp.where(mask, c0 - r0, -jnp.inf))
        l1 = jnp.exp(jnp.where(mask, c1 - r1, -jnp.inf))
        m = jnp.concatenate([scores * (l0 * d0), scores * (l1 * d1)], axis=1).astype(BF16)
        xs = x[:, sl]
        xb = jnp.concatenate([jnp.where(low, xs, 0.0), jnp.where(low, 0.0, xs)], axis=0).astype(BF16)
        css = jnp.where(low, c0, c1)
        y = _dot(m, xb) + y_inter[:, sl] * jnp.exp(css)
        if d == 0:
            y = y + xs * dsk[:, sl]
        slabs.append(y)
        w0 = d0 * jnp.exp(r0[:, edge:edge + 1] - r0)
        w1 = d1 * jnp.exp(r1[:, edge:edge + 1] - r1)
        bw = jnp.concatenate([bt * w0, bt * w1], axis=1).astype(BF16)
        st_ref[d, gi, :, sl] = st[:, sl] * jnp.exp(css[edge:edge + 1, :]) + _dot(bw, xb)
    return jnp.concatenate(slabs, axis=1)


def _ssd_kernel(xf_ref, bf_ref, cf_ref, dtf_ref, csf_ref, cstf_ref,
                xb_ref, bb_ref, cb_ref, dtb_ref, csb_ref, cstb_ref,
                dsk_ref, h0_ref, y_ref, ht_ref, st_ref):
    gs = _SSD_GROUPS_PER_STEP
    g0 = pl.program_id(1) * gs
    c = pl.program_id(2)
    nc = pl.num_programs(2)

    @pl.when(c == 0)
    def _():
        st_ref[...] = h0_ref[0]

    yf = jnp.concatenate([_ssd_direction(0, g0 + gi, gi, xf_ref, bf_ref, cf_ref, dtf_ref, csf_ref, cstf_ref,
                                         dsk_ref, st_ref) for gi in range(gs)], axis=1)
    yb = jnp.concatenate([_ssd_direction(1, g0 + gi, gi, xb_ref, bb_ref, cb_ref, dtb_ref, csb_ref, cstb_ref,
                                         dsk_ref, st_ref) for gi in range(gs)], axis=1)
    rows_f = pl.ds(pl.multiple_of(c * SSD_CHUNK, SSD_CHUNK), SSD_CHUNK)
    rows_b = pl.ds(pl.multiple_of((nc - 1 - c) * SSD_CHUNK, SSD_CHUNK), SSD_CHUNK)

    @pl.when(c < nc // 2)
    def _():
        y_ref[0, rows_f, :] = yf
        y_ref[0, rows_b, :] = yb

    @pl.when(c >= nc // 2)
    def _():
        y_ref[0, rows_f, :] += yf
        y_ref[0, rows_b, :] += yb

    @pl.when(c == nc - 1)
    def _():
        ht_ref[0] = st_ref[...]


def _ssd_scan(xbc, dtt, cs, cst, d_skip_row, h0):
    nb, l, _ = xbc.shape
    nc = l // SSD_CHUNK
    assert nc % 2 == 0
    n = SSD_CHUNK
    d_inner = SSD_GROUPS * _GROUP_W
    b_blk0 = d_inner // SSD_STATE
    c_blk0 = b_blk0 + SSD_GROUPS
    nh2 = cs.shape[2]
    gs = _SSD_GROUPS_PER_STEP
    assert SSD_GROUPS % gs == 0 and b_blk0 % gs == 0 and c_blk0 % gs == 0
    d_skip_row = d_skip_row.reshape(SSD_GROUPS // gs, 1, gs * _GROUP_W)

    def chunk_specs(which):
        return [pl.BlockSpec((1, n, gs * _GROUP_W), lambda b, g, c: (b, which(c), g)),
                pl.BlockSpec((1, n, gs * SSD_STATE), lambda b, g, c: (b, which(c), b_blk0 // gs + g)),
                pl.BlockSpec((1, n, gs * SSD_STATE), lambda b, g, c: (b, which(c), c_blk0 // gs + g)),
                pl.BlockSpec((1, nh2, n), lambda b, g, c: (b, 0, which(c))),
                pl.BlockSpec((1, n, nh2), lambda b, g, c: (b, which(c), 0)),
                pl.BlockSpec((1, nh2, n), lambda b, g, c: (b, 0, which(c)))]

    state_shape = jax.ShapeDtypeStruct(h0.shape, F32)
    return pl.pallas_call(
        _ssd_kernel,
        grid=(nb, SSD_GROUPS // gs, nc),
        in_specs=(chunk_specs(lambda c: c) + chunk_specs(lambda c: nc - 1 - c)
                  + [pl.BlockSpec((1, 1, gs * _GROUP_W), lambda b, g, c: (g, 0, 0)),
                     pl.BlockSpec((1, 2, gs, SSD_STATE, _GROUP_W), lambda b, g, c: (b, 0, g, 0, 0))]),
        out_specs=[pl.BlockSpec((1, l, gs * _GROUP_W), lambda b, g, c: (b, 0, g)),
                   pl.BlockSpec((1, 2, gs, SSD_STATE, _GROUP_W), lambda b, g, c: (b, 0, g, 0, 0))],
        out_shape=[jax.ShapeDtypeStruct((nb, l, d_inner), F32), state_shape],
        scratch_shapes=[pltpu.VMEM((2, gs, SSD_STATE, _GROUP_W), F32)],
        compiler_params=_params("parallel", "parallel", "arbitrary"),
        name="ssd_scan",
    )(xbc, xbc, xbc, dtt, cs, cst, xbc, xbc, xbc, dtt, cs, cst, d_skip_row, h0)


_LRU_CT = 512
_LRU_ROWS = 128


def _tile_scan(a, u, h_in, reverse):
    rows = lax.broadcasted_iota(jnp.int32, a.shape, 0)
    for s in (1, 2, 4):
        if reverse:
            keep = rows < SUBLANES - s
            amount = SUBLANES - s
        else:
            keep = rows >= s
            amount = s
        a_s = jnp.where(keep, pltpu.roll(a, amount, 0), 1.0)
        u_s = jnp.where(keep, pltpu.roll(u, amount, 0), 0.0)
        u = a * u_s + u
        a = a * a_s
    h = a * h_in + u
    return h, (h[0:1] if reverse else h[SUBLANES - 1:SUBLANES])


def _lru_kernel(rec_ref, w_ref, ba_ref, bi_ref, ap_ref, h0_ref, o_ref, ht_ref, a_scr, u_scr):
    l = rec_ref.shape[1]
    nch = l // _LRU_ROWS
    n_tiles = _LRU_ROWS // SUBLANES
    n_blk = _LRU_CT // LRU_BLOCK_W

    def gates(d, start):
        rec = rec_ref[0, pl.ds(start, _LRU_ROWS), :]
        rate = _softplus(-ap_ref[d]) * (-LRU_C * math.log2(math.e))
        for k in range(n_blk):
            sl = slice(k * LRU_BLOCK_W, (k + 1) * LRU_BLOCK_W)
            rk = rec[:, sl]
            pre = _dot(rk.astype(BF16), w_ref[d, k])
            r = _sigmoid(pre[:, :LRU_BLOCK_W] + ba_ref[d][:, sl])
            ig = _sigmoid(pre[:, LRU_BLOCK_W:] + bi_ref[d][:, sl])
            a = jnp.exp2(r * rate[:, sl])
            a_scr[d, :, sl] = a
            u_scr[d, :, sl] = jnp.sqrt(1.0 - a * a) * (ig * rk)

    def chunk(c, carry, accumulate):
        hf, hb = carry
        start_f = pl.multiple_of(c * _LRU_ROWS, _LRU_ROWS)
        start_b = pl.multiple_of((nch - 1 - c) * _LRU_ROWS, _LRU_ROWS)
        gates(0, start_f)
        gates(1, start_b)
        for t in range(n_tiles):
            tf = t * SUBLANES
            tb = (n_tiles - 1 - t) * SUBLANES
            hs_f, hf = _tile_scan(a_scr[0, tf:tf + SUBLANES, :], u_scr[0, tf:tf + SUBLANES, :], hf, False)
            hs_b, hb = _tile_scan(a_scr[1, tb:tb + SUBLANES, :], u_scr[1, tb:tb + SUBLANES, :], hb, True)
            rows_f = pl.ds(pl.multiple_of(start_f + tf, SUBLANES), SUBLANES)
            rows_b = pl.ds(pl.multiple_of(start_b + tb, SUBLANES), SUBLANES)
            if accumulate:
                o_ref[0, rows_f, :] += hs_f
                o_ref[0, rows_b, :] += hs_b
            else:
                o_ref[0, rows_f, :] = hs_f
                o_ref[0, rows_b, :] = hs_b
        return hf, hb

    carry = (h0_ref[0, 0], h0_ref[0, 1])
    carry = lax.fori_loop(0, nch // 2, lambda c, cr: chunk(c, cr, False), carry)
    hf, hb = lax.fori_loop(nch // 2, nch, lambda c, cr: chunk(c, cr, True), carry)
    ht_ref[0, 0] = hf
    ht_ref[0, 1] = hb


def _lru_scan(rec, w_gate, b_a, b_i, a_param, h0):
    nb, l, w = rec.shape
    assert (l // _LRU_ROWS) % 2 == 0
    n_blk = _LRU_CT // LRU_BLOCK_W
    vec = pl.BlockSpec((2, 1, _LRU_CT), lambda b, j: (0, 0, j))
    return pl.pallas_call(
        _lru_kernel,
        grid=(nb, w // _LRU_CT),
        in_specs=[pl.BlockSpec((1, l, _LRU_CT), lambda b, j: (b, 0, j)),
                  pl.BlockSpec((2, n_blk, LRU_BLOCK_W, 2 * LRU_BLOCK_W), lambda b, j: (0, j, 0, 0)),
                  vec, vec, vec,
                  pl.BlockSpec((1, 2, 1, _LRU_CT), lambda b, j: (b, 0, 0, j))],
        out_specs=[pl.BlockSpec((1, l, _LRU_CT), lambda b, j: (b, 0, j)),
                   pl.BlockSpec((1, 2, 1, _LRU_CT), lambda b, j: (b, 0, 0, j))],
        out_shape=[jax.ShapeDtypeStruct((nb, l, w), F32), jax.ShapeDtypeStruct((nb, 2, 1, w), F32)],
        scratch_shapes=[pltpu.VMEM((2, _LRU_ROWS, _LRU_CT), F32), pltpu.VMEM((2, _LRU_ROWS, _LRU_CT), F32)],
        compiler_params=_params("parallel", "parallel"),
        name="lru_scan",
    )(rec, w_gate, b_a, b_i, a_param, h0)


def _rope_tables(length):
    pos = jnp.arange(length)
    row = (pos // GRID_W).astype(F32)
    col = (pos % GRID_W).astype(F32)
    n_freq = HEAD_DIM // 4
    inv = ROPE_THETA ** (-jnp.arange(n_freq, dtype=F32) / n_freq)
    ang = jnp.concatenate([row[:, None] * inv, col[:, None] * inv], axis=-1)
    cos, sin = jnp.cos(ang), jnp.sin(ang)
    return jnp.concatenate([cos, cos], axis=-1), jnp.concatenate([-sin, sin], axis=-1)


def _attn_mixer(x, seq_shape, mods, nw, w_qkv, head_norms, w_o, rope, k_ctx, v_ctx):
    shift, scale, gate = mods
    nb, l, d = x.shape
    b, s = seq_shape
    use_rope = rope is not None
    cos_t, sin_t = rope if use_rope else (jnp.zeros((l, HEAD_DIM), F32), jnp.zeros((l, HEAD_DIM), F32))
    qkv = _qkv_proj(x, nw, shift, scale, w_qkv, head_norms, cos_t, sin_t, rope=use_rope,
                    out_dtype=BF16 if use_rope else F32)
    qkv_seq = qkv.reshape(b, s, qkv.shape[-1])
    o = _attention(qkv_seq, k_ctx, v_ctx)
    x = _mixer_out("attn", [o.reshape(nb, l, d)], w_o, x, gate)
    return x, qkv_seq


def _ssd_mixer(x, seq_shape, mods, nw, w_in, conv_w, conv_b, dt_bias, a_log, d_skip_row, norm_w, w_out, h0):
    shift, scale, gate = mods
    nb, l, d = x.shape
    b, s = seq_shape
    d_inner = w_out.shape[0]
    conv_dim = conv_w.shape[1]
    proj = _norm_mod_matmul(x, nw, shift, scale, w_in, tn=_SSD_IN_TN, name="ssd_in")
    proj_seq = proj.reshape(b, s, proj.shape[-1])
    xbc = _dwconv(proj_seq, d_inner, conv_dim, conv_w, conv_b, silu=True)
    dt, cs = _ssd_dt(proj_seq, d_inner + conv_dim, dt_bias, a_log)
    y, h_t = _ssd_scan(xbc, jnp.swapaxes(dt, 1, 2), cs, jnp.swapaxes(cs, 1, 2), d_skip_row, h0)
    hn = _gated_norm(y.reshape(nb, l, d_inner), proj, norm_w)
    x = _mixer_out("ssd", [hn], w_out, x, gate, tm=512)
    return x, h_t


def _lru_mixer(x, seq_shape, mods, nw, w_in, conv_w, conv_b, w_gate, b_a, b_i, a_param, w_out, h0):
    shift, scale, gate = mods
    nb, l, d = x.shape
    b, s = seq_shape
    width = w_out.shape[0]
    proj = _norm_mod_matmul(x, nw, shift, scale, w_in, tn=1024, name="lru_in")
    rec = _dwconv(proj.reshape(b, s, 2 * width), width, width, conv_w, conv_b, silu=False)
    hs, h_t = _lru_scan(rec, w_gate, b_a, b_i, a_param, h0)
    x = _mixer_out("lru", [hs.reshape(nb, l, width), proj], w_out, x, gate, tm=512, tn=d)
    return x, h_t


def _states_to_kernel_layout(h):
    b, nh, p, n = h.shape
    return h.reshape(b, SSD_GROUPS, nh // SSD_GROUPS, p, n).transpose(0, 1, 4, 2, 3).reshape(b, SSD_GROUPS, n, -1)


def _states_from_kernel_layout(h, nh, p):
    b, two, g, n, _ = h.shape
    return h.reshape(b, two, g, n, nh // g, p).transpose(0, 1, 2, 4, 5, 3).reshape(b, two, nh, p, n)


def kernel(x_prompt, x_sample, cache_k, cache_v, state_ssm, state_lru, c, c_ctx, ada_w, ada_b, norm_mix, norm_ffn, attn_w_qkv, attn_q_norm, attn_k_norm, attn_w_o, ssd_w_in, ssd_conv_w, ssd_conv_b, ssd_dt_bias, ssd_a_log, ssd_d, ssd_norm, ssd_w_out, lru_w_in, lru_conv_w, lru_conv_b, lru_w_a, lru_b_a, lru_w_i, lru_b_i, lru_a_param, lru_w_out, ffn_w_in, ffn_w_out, final_norm):
    bp, sp, d = x_prompt.shape
    bs, ss, _ = x_sample.shape
    depth = ada_w.shape[0]
    n_mixers = 3

    pad_rows = -(bs + 1) % SUBLANES
    cond = jnp.concatenate([c, c_ctx[None, :], jnp.zeros((pad_rows, d), F32)], axis=0)
    mod = _adaln(cond, ada_w, ada_b).reshape(depth, bs + 1 + pad_rows, 6, 1, d)

    xp = x_prompt.reshape(1, bp * sp, d)
    xs = x_sample
    streams = {"p": (bp, sp), "s": (bs, ss)}
    rope = _rope_tables(ss)

    ssd_nh = ssd_d.shape[1]
    ssd_p = state_ssm.shape[4]
    new_k, new_v, new_ssm, new_lru = [], [], [], []
    for i in range(depth):
        kind, j = i % n_mixers, i // n_mixers
        mp = [mod[i, bs:bs + 1, t] for t in range(6)]
        ms = [mod[i, :bs, t] for t in range(6)]
        if kind == 0:
            w_qkv = attn_w_qkv[j].astype(BF16)
            w_o = attn_w_o[j].astype(BF16)
            head_norms = jnp.stack([attn_q_norm[j], attn_k_norm[j]]).reshape(2, 1, HEAD_DIM)
            kv_w = N_KV_HEADS * HEAD_DIM
            xp, qkv_p = _attn_mixer(xp, streams["p"], mp[:3], norm_mix[i], w_qkv, head_norms, w_o, None, None, None)
            xs, _ = _attn_mixer(xs, streams["s"], ms[:3], norm_mix[i], w_qkv, head_norms, w_o, rope,
                                cache_k[:, j], cache_v[:, j])
            q_w = qkv_p.shape[-1] - 2 * kv_w
            new_k.append(qkv_p[:, :, q_w:q_w + kv_w].reshape(bp, sp, N_KV_HEADS, HEAD_DIM))
            new_v.append(qkv_p[:, :, q_w + kv_w:].reshape(bp, sp, N_KV_HEADS, HEAD_DIM))
        elif kind == 1:
            w_in = ssd_w_in[j].astype(BF16)
            w_in = jnp.pad(w_in, ((0, 0), (0, -w_in.shape[1] % _SSD_IN_TN)))
            w_out = ssd_w_out[j].astype(BF16)
            d_skip_row = jnp.repeat(ssd_d[j], ssd_p).reshape(SSD_GROUPS, 1, _GROUP_W)
            args = (norm_mix[i], w_in, ssd_conv_w[j], ssd_conv_b[j], ssd_dt_bias[j], ssd_a_log[j], d_skip_row,
                    ssd_norm[j], w_out)
            zero = jnp.zeros((bp, 2, SSD_GROUPS, SSD_STATE, _GROUP_W), F32)
            h0 = jnp.stack([_states_to_kernel_layout(state_ssm[:, j, 0]),
                            _states_to_kernel_layout(state_ssm[:, j, 1])], axis=1)
            xp, h_t = _ssd_mixer(xp, streams["p"], mp[:3], *args, zero)
            xs, _ = _ssd_mixer(xs, streams["s"], ms[:3], *args, h0)
            new_ssm.append(_states_from_kernel_layout(h_t, ssd_nh, ssd_p))
        else:
            w_in = lru_w_in[j].astype(BF16)
            w_out = lru_w_out[j].astype(BF16)
            width = w_out.shape[0]
            w_gate = jnp.concatenate([lru_w_a[j], lru_w_i[j]], axis=-1).astype(BF16)
            args = (norm_mix[i], w_in, lru_conv_w[j], lru_conv_b[j], w_gate, lru_b_a[j].reshape(2, 1, width),
                    lru_b_i[j].reshape(2, 1, width), lru_a_param[j].reshape(2, 1, width), w_out)
            xp, h_t = _lru_mixer(xp, streams["p"], mp[:3], *args, jnp.zeros((bp, 2, 1, width), F32))
            xs, _ = _lru_mixer(xs, streams["s"], ms[:3], *args, state_lru[:, j].reshape(bs, 2, 1, width))
            new_lru.append(h_t.reshape(bp, 2, width))
        w1 = ffn_w_in[i].astype(BF16)
        w2 = ffn_w_out[i].astype(BF16)
        last = i == depth - 1
        xp = _ffn(xp, norm_ffn[i], mp[3], mp[4], mp[5], w1, w2, final_norm, final_norm=last)
        xs = _ffn(xs, norm_ffn[i], ms[3], ms[4], ms[5], w1, w2, final_norm, final_norm=last)
    y_prompt = xp.reshape(bp, sp, d)
    y_sample = xs
    return (y_prompt, y_sample, jnp.stack(new_k, axis=1), jnp.stack(new_v, axis=1),
            jnp.stack(new_ssm, axis=1), jnp.stack(new_lru, axis=1))
```
